```python
import functools
import jax, jax.numpy as jnp
from jax import lax
import numpy as np

D_MODEL = 2048
BATCH = 4
SEQ = 2048
DEPTH = 2
DEC_BATCH = 8
DEC_SEQ = 8
PAST_LEN = 16384
PAGE_SIZE = 128

HGRN_HEADS = 16
HGRN_DK = 128
HGRN_DV = D_MODEL // HGRN_HEADS
HGRN_CHUNK = 64
CONV_CHANNELS = D_MODEL
CONV_WIDTH = 31
FOX_HEADS = 16
FOX_HEAD_DIM = D_MODEL // FOX_HEADS
Q_BLOCK = 128
N_BRANCH = 3
N_MEM = 256
MEM_HEADS = 4
MEM_HEAD_DIM = D_MODEL // MEM_HEADS
D_FF = 11 * D_MODEL // 4
FFN_CONV_WIDTH = 3
EPS = 1e-6
FORGET_BIAS_INIT = 2.0
CACHE_FORGET_BIAS = 12.0
IN_WIDTHS = (HGRN_HEADS * HGRN_DK, HGRN_HEADS * HGRN_DK, HGRN_HEADS * HGRN_DV, HGRN_HEADS * HGRN_DV,
             2 * CONV_CHANNELS,
             FOX_HEADS * FOX_HEAD_DIM, FOX_HEADS * FOX_HEAD_DIM, FOX_HEADS * FOX_HEAD_DIM, FOX_HEADS,
             N_BRANCH * D_MODEL)
D_IN = sum(IN_WIDTHS)

kernel_name = "hybrid_hgrn2_conformer_fox_decode_step"


def rmsnorm(x, g):
    xf = x.astype(jnp.float32)
    y = xf * lax.rsqrt(jnp.mean(xf * xf, axis=-1, keepdims=True) + EPS)
    return (y * g.astype(jnp.float32)).astype(x.dtype)


def layernorm(x, g, b):
    xf = x.astype(jnp.float32)
    mu = jnp.mean(xf, axis=-1, keepdims=True)
    var = jnp.mean(jnp.square(xf - mu), axis=-1, keepdims=True)
    y = (xf - mu) * lax.rsqrt(var + EPS)
    return (y * g.astype(jnp.float32) + b.astype(jnp.float32)).astype(x.dtype)


def causal_dwconv(x, buf, w, b):
    width = w.shape[0]
    xp = jnp.concatenate([buf.astype(x.dtype), x], axis=1)
    y = lax.conv_general_dilated(xp, w[:, None, :].astype(x.dtype), window_strides=(1,), padding='VALID',
                                 dimension_numbers=('NWC', 'WIO', 'NWC'), feature_group_count=x.shape[-1])
    return y + b.astype(x.dtype), xp[:, xp.shape[1] - (width - 1):]


def split_columns(z):
    offsets = []
    acc = 0
    for w in IN_WIDTHS[:-1]:
        acc += w
        offsets.append(acc)
    return jnp.split(z, offsets, axis=-1)


def hgrn_chunk(S, q, k, v, g):
    L = q.shape[1]
    G = jnp.cumsum(g, axis=1)
    o_inter = jnp.einsum('blhk,bhkv->blhv', q * jnp.exp(G), S)
    causal = jnp.tril(jnp.ones((L, L), dtype=bool))[None, :, :, None, None]
    decay = jnp.exp(jnp.where(causal, G[:, :, None] - G[:, None, :], -jnp.inf))
    A = jnp.einsum('bthk,bshk,btshk->bhts', q, k, decay)
    o_intra = jnp.einsum('bhts,bshv->bthv', A, v)
    GL = G[:, -1]
    S_new = jnp.exp(GL)[..., None] * S + jnp.einsum('bshk,bshv->bhkv', k * jnp.exp(GL[:, None] - G), v)
    return S_new, o_inter + o_intra


def hgrn2(q, k, v, g, s0):
    b, t = q.shape[:2]
    L = HGRN_CHUNK if t % HGRN_CHUNK == 0 else t
    n = t // L

    def to_chunks(a):
        return jnp.moveaxis(a.reshape((b, n, L) + a.shape[2:]), 1, 0)

    s_fin, o = lax.scan(lambda S, xs: hgrn_chunk(S, *xs), s0, (to_chunks(q), to_chunks(k), to_chunks(v), to_chunks(g)))
    o = jnp.moveaxis(o, 0, 1).reshape((b, t) + o.shape[3:])
    return o, s_fin


def fox_prompt(q, k, v, logf):
    b, t, h, dh = q.shape
    nb = t // Q_BLOCK
    scale = dh ** -0.5
    cT = jnp.moveaxis(jnp.cumsum(logf, axis=1), 2, 1)
    kpos = jnp.arange(t)

    def block(args):
        qb, cb, i = args
        qpos = i * Q_BLOCK + jnp.arange(Q_BLOCK)
        s = jnp.einsum('bqhd,bkhd->bhqk', qb, k, preferred_element_type=jnp.float32) * scale
        s = s + cb[..., :, None] - cT[:, :, None, :]
        s = jnp.where(kpos[None, :] <= qpos[:, None], s, -jnp.inf)
        p = jax.nn.softmax(s, axis=-1)
        return jnp.einsum('bhqk,bkhd->bqhd', p.astype(v.dtype), v)

    qb = jnp.moveaxis(q.reshape(b, nb, Q_BLOCK, h, dh), 1, 0)
    cb = jnp.moveaxis(cT.reshape(b, h, nb, Q_BLOCK), 2, 0)
    o = lax.map(block, (qb, cb, jnp.arange(nb)))
    return jnp.moveaxis(o, 0, 1).reshape(b, t, h, dh)


def fox_sample(q, k, v, logf, k_past, v_past, logf_past):
    t = q.shape[1]
    P = k_past.shape[1]
    scale = q.shape[-1] ** -0.5
    d_past = lax.cumsum(logf_past, axis=1, reverse=True) - logf_past
    cnT = jnp.moveaxis(jnp.cumsum(logf, axis=1), 2, 1)
    s_past = jnp.einsum('bqhd,bkhd->bhqk', q, k_past.astype(q.dtype), preferred_element_type=jnp.float32) * scale
    s_past = s_past + jnp.moveaxis(d_past, 2, 1)[:, :, None, :] + cnT[..., :, None]
    s_new = jnp.einsum('bqhd,bkhd->bhqk', q, k, preferred_element_type=jnp.float32) * scale
    s_new = s_new + cnT[..., :, None] - cnT[..., None, :]
    s_new = jnp.where(jnp.tril(jnp.ones((t, t), dtype=bool)), s_new, -jnp.inf)
    p = jax.nn.softmax(jnp.concatenate([s_past, s_new], axis=-1), axis=-1).astype(v.dtype)
    return (jnp.einsum('bhqk,bkhd->bqhd', p[..., :P], v_past.astype(v.dtype))
            + jnp.einsum('bhqk,bkhd->bqhd', p[..., P:], v))


def memory_kv(mem, g, w_k, w_v):
    b, m, _ = mem.shape
    mn = rmsnorm(mem, g)
    return ((mn @ w_k).reshape(b, m, MEM_HEADS, MEM_HEAD_DIM), (mn @ w_v).reshape(b, m, MEM_HEADS, MEM_HEAD_DIM))


def trunk_layer(x, mem_k, mem_v, hgrn_s0, conv_buf, ffn_buf, attend, lb, p):
    b, t, _ = x.shape
    f32 = jnp.float32
    h = rmsnorm(x, p['norm_mix_g'])
    z = h @ p['w_in']
    a_q, a_f, a_i, a_g, b_u, c_q, c_k, c_v, c_f, gate_logits = split_columns(z)
    q = jax.nn.silu(a_q.astype(f32)).reshape(b, t, HGRN_HEADS, HGRN_DK)
    fg = lb + (1.0 - lb) * jax.nn.sigmoid(a_f.astype(f32))
    kk = (1.0 - fg).reshape(b, t, HGRN_HEADS, HGRN_DK)
    gl = jnp.log(fg).reshape(b, t, HGRN_HEADS, HGRN_DK)
    vi = a_i.astype(f32).reshape(b, t, HGRN_HEADS, HGRN_DV)
    o_a, s_new = hgrn2(q, kk, vi, gl, hgrn_s0.astype(f32))
    o_a = rmsnorm(o_a, p['hgrn_norm_g']) * jax.nn.sigmoid(a_g.astype(f32).reshape(b, t, HGRN_HEADS, HGRN_DV))
    y_a = o_a.reshape(b, t, -1).astype(x.dtype) @ p['w_branch_a']
    u = b_u[..., :CONV_CHANNELS] * jax.nn.sigmoid(b_u[..., CONV_CHANNELS:])
    uc, conv_new = causal_dwconv(u, conv_buf, p['conv_dw_w'], p['conv_dw_b'])
    uc = jax.nn.silu(layernorm(uc, p['conv_ln_g'], p['conv_ln_b']))
    y_b = uc @ p['w_branch_b']
    qc = c_q.reshape(b, t, FOX_HEADS, FOX_HEAD_DIM)
    kc = c_k.reshape(b, t, FOX_HEADS, FOX_HEAD_DIM)
    vc = c_v.reshape(b, t, FOX_HEADS, FOX_HEAD_DIM)
    logf = jax.nn.log_sigmoid(c_f.astype(f32) + p['fox_f_bias'].astype(f32))
    o_c = attend(qc, kc, vc, logf)
    y_c = o_c.reshape(b, t, -1).astype(x.dtype) @ p['w_branch_c']
    g_a, g_b, g_c = jnp.split(jax.nn.sigmoid(gate_logits), N_BRANCH, axis=-1)
    x = x + (g_a * y_a + g_b * y_b + g_c * y_c) @ p['w_out']
    h = rmsnorm(x, p['norm_mem_g'])
    qm = (h @ p['w_mq']).reshape(b, t, MEM_HEADS, MEM_HEAD_DIM)
    sm = jnp.einsum('bthd,bmhd->bhtm', qm, mem_k.astype(x.dtype), preferred_element_type=f32) * MEM_HEAD_DIM ** -0.5
    pm = jax.nn.softmax(sm, axis=-1).astype(x.dtype)
    om = jnp.einsum('bhtm,bmhd->bthd', pm, mem_v.astype(x.dtype)).reshape(b, t, -1)
    x = x + om @ p['w_mo']
    h = rmsnorm(x, p['norm_ffn_g'])
    up = h @ p['w_up']
    gate, val = up[..., :D_FF], up[..., D_FF:]
    gate_c, ffn_new = causal_dwconv(gate, ffn_buf, p['ffn_dw_w'], p['ffn_dw_b'])
    x = x + (jax.nn.silu(gate_c) * val) @ p['w_down']
    return x, (kc, vc, logf, s_new, conv_new, ffn_new)


def setup_inputs(seed: int = 0) -> dict:
    key = jax.random.key(seed)
    keys = jax.random.split(key, 40)
    f32 = jnp.float32

    def nrm(i, shape, scale=1.0):
        return scale * jax.random.normal(keys[i], shape, f32)

    d = D_MODEL
    n_pages = PAST_LEN // PAGE_SIZE
    n_pool = (5 * DEC_BATCH * n_pages + 3) // 4
    page_table = jax.random.permutation(keys[0], n_pool)[: DEC_BATCH * n_pages].reshape(DEC_BATCH, n_pages).astype(jnp.int32)
    wa = HGRN_HEADS * HGRN_DV
    wc = FOX_HEADS * FOX_HEAD_DIM
    wm = MEM_HEADS * MEM_HEAD_DIM
    return {
        'x_prompt': nrm(1, (BATCH, SEQ, d)),
        'x_sample': nrm(2, (DEC_BATCH, DEC_SEQ, d)),
        'cache_k': nrm(3, (DEPTH, n_pool, PAGE_SIZE, FOX_HEADS, FOX_HEAD_DIM)),
        'cache_v': nrm(4, (DEPTH, n_pool, PAGE_SIZE, FOX_HEADS, FOX_HEAD_DIM)),
        'cache_logf': jax.nn.log_sigmoid(CACHE_FORGET_BIAS + nrm(5, (DEPTH, n_pool, PAGE_SIZE, FOX_HEADS))),
        'page_table': page_table,
        'cache_mem_k': nrm(6, (DEPTH, DEC_BATCH, N_MEM, MEM_HEADS, MEM_HEAD_DIM)),
        'cache_mem_v': nrm(7, (DEPTH, DEC_BATCH, N_MEM, MEM_HEADS, MEM_HEAD_DIM)),
        'state_hgrn': nrm(8, (DEPTH, DEC_BATCH, HGRN_HEADS, HGRN_DK, HGRN_DV), 0.5),
        'state_conv': nrm(9, (DEPTH, DEC_BATCH, CONV_WIDTH - 1, CONV_CHANNELS), 0.5),
        'state_ffn': nrm(10, (DEPTH, DEC_BATCH, FFN_CONV_WIDTH - 1, D_FF)),
        'mem_prompt': nrm(11, (BATCH, N_MEM, d)),
        'norm_mix_g': 1.0 + nrm(12, (DEPTH, d), 0.02),
        'w_in': nrm(13, (DEPTH, d, D_IN), d ** -0.5),
        'hgrn_lb_logits': nrm(14, (DEPTH, HGRN_HEADS * HGRN_DK), 0.5),
        'hgrn_norm_g': 1.0 + nrm(15, (DEPTH, HGRN_DV), 0.02),
        'conv_dw_w': nrm(16, (DEPTH, CONV_WIDTH, CONV_CHANNELS), CONV_WIDTH ** -0.5),
        'conv_dw_b': nrm(17, (DEPTH, CONV_CHANNELS), 0.02),
        'conv_ln_g': 1.0 + nrm(18, (DEPTH, CONV_CHANNELS), 0.02),
        'conv_ln_b': nrm(19, (DEPTH, CONV_CHANNELS), 0.02),
        'fox_f_bias': FORGET_BIAS_INIT + nrm(20, (DEPTH, FOX_HEADS), 0.2),
        'w_branch_a': nrm(21, (DEPTH, wa, d), wa ** -0.5),
        'w_branch_b': nrm(22, (DEPTH, CONV_CHANNELS, d), CONV_CHANNELS ** -0.5),
        'w_branch_c': nrm(23, (DEPTH, wc, d), wc ** -0.5),
        'w_out': nrm(24, (DEPTH, d, d), d ** -0.5),
        'norm_mem_g': 1.0 + nrm(25, (DEPTH, d), 0.02),
        'mem_kv_norm_g': 1.0 + nrm(26, (DEPTH, d), 0.02),
        'w_mq': nrm(27, (DEPTH, d, wm), d ** -0.5),
        'w_mk': nrm(28, (DEPTH, d, wm), d ** -0.5),
        'w_mv': nrm(29, (DEPTH, d, wm), d ** -0.5),
        'w_mo': nrm(30, (DEPTH, wm, d), wm ** -0.5),
        'norm_ffn_g': 1.0 + nrm(31, (DEPTH, d), 0.02),
        'w_up': nrm(32, (DEPTH, d, 2 * D_FF), d ** -0.5),
        'ffn_dw_w': nrm(33, (DEPTH, FFN_CONV_WIDTH, D_FF), FFN_CONV_WIDTH ** -0.5),
        'ffn_dw_b': nrm(34, (DEPTH, D_FF), 0.02),
        'w_down': nrm(35, (DEPTH, D_FF, d), D_FF ** -0.5),
        'final_norm_g': 1.0 + nrm(36, (d,), 0.02),
    }


def _stack(rows, i):
    return jnp.stack([r[i] for r in rows])


def reference(x_prompt, x_sample, cache_k, cache_v, cache_logf, page_table, cache_mem_k, cache_mem_v,
              state_hgrn, state_conv, state_ffn, mem_prompt, norm_mix_g, w_in, hgrn_lb_logits, hgrn_norm_g,
              conv_dw_w, conv_dw_b, conv_ln_g, conv_ln_b, fox_f_bias, w_branch_a, w_branch_b, w_branch_c, w_out,
              norm_mem_g, mem_kv_norm_g, w_mq, w_mk, w_mv, w_mo, norm_ffn_g, w_up, ffn_dw_w, ffn_dw_b, w_down,
              final_norm_g):
    lb_soft = jax.nn.softmax(hgrn_lb_logits.astype(jnp.float32), axis=0)
    lower_bounds = jnp.cumsum(lb_soft, axis=0) - lb_soft[0]
    bp = x_prompt.shape[0]
    dbatch = x_sample.shape[0]
    n_pages = page_table.shape[1]
    past = n_pages * PAGE_SIZE
    xp, xs = x_prompt, x_sample
    rows_p, rows_s, mem_rows = [], [], []
    for l in range(DEPTH):
        p = {'norm_mix_g': norm_mix_g[l], 'w_in': w_in[l], 'hgrn_norm_g': hgrn_norm_g[l],
             'conv_dw_w': conv_dw_w[l], 'conv_dw_b': conv_dw_b[l], 'conv_ln_g': conv_ln_g[l], 'conv_ln_b': conv_ln_b[l],
             'fox_f_bias': fox_f_bias[l], 'w_branch_a': w_branch_a[l], 'w_branch_b': w_branch_b[l],
             'w_branch_c': w_branch_c[l], 'w_out': w_out[l], 'norm_mem_g': norm_mem_g[l], 'w_mq': w_mq[l],
             'w_mo': w_mo[l], 'norm_ffn_g': norm_ffn_g[l], 'w_up': w_up[l], 'ffn_dw_w': ffn_dw_w[l],
             'ffn_dw_b': ffn_dw_b[l], 'w_down': w_down[l]}
        lb = lower_bounds[l]
        mk, mv = memory_kv(mem_prompt, mem_kv_norm_g[l], w_mk[l], w_mv[l])
        mem_rows.append((mk, mv))
        s0 = jnp.zeros((bp, HGRN_HEADS, HGRN_DK, HGRN_DV), jnp.float32)
        cb0 = jnp.zeros((bp, CONV_WIDTH - 1, CONV_CHANNELS), xp.dtype)
        fb0 = jnp.zeros((bp, FFN_CONV_WIDTH - 1, D_FF), xp.dtype)
        xp, new_p = trunk_layer(xp, mk, mv, s0, cb0, fb0, fox_prompt, lb, p)
        rows_p.append(new_p)
        kpast = cache_k[l][page_table].reshape(dbatch, past, FOX_HEADS, FOX_HEAD_DIM)
        vpast = cache_v[l][page_table].reshape(dbatch, past, FOX_HEADS, FOX_HEAD_DIM)
        lfpast = cache_logf[l][page_table].reshape(dbatch, past, FOX_HEADS).astype(jnp.float32)
        attend_s = functools.partial(fox_sample, k_past=kpast, v_past=vpast, logf_past=lfpast)
        xs, new_s = trunk_layer(xs, cache_mem_k[l], cache_mem_v[l], state_hgrn[l], state_conv[l], state_ffn[l],
                                attend_s, lb, p)
        rows_s.append(new_s)
    y_prompt = rmsnorm(xp, final_norm_g)
    y_sample = rmsnorm(xs, final_norm_g)
    k_prompt = _stack(rows_p, 0)
    v_prompt = _stack(rows_p, 1)
    logf_prompt = _stack(rows_p, 2)
    hgrn_prompt = _stack(rows_p, 3)
    conv_prompt = _stack(rows_p, 4)
    ffn_prompt = _stack(rows_p, 5)
    mem_k_prompt = _stack(mem_rows, 0)
    mem_v_prompt = _stack(mem_rows, 1)
    k_sample = _stack(rows_s, 0)
    v_sample = _stack(rows_s, 1)
    logf_sample = _stack(rows_s, 2)
    hgrn_sample = _stack(rows_s, 3)
    conv_sample = _stack(rows_s, 4)
    ffn_sample = _stack(rows_s, 5)
    return (y_prompt, y_sample, k_prompt, v_prompt, logf_prompt, mem_k_prompt, mem_v_prompt, hgrn_prompt,
            conv_prompt, ffn_prompt, k_sample, v_sample, logf_sample, hgrn_sample, conv_sample, ffn_sample)
```

```python
import functools

import jax
import jax.numpy as jnp
from jax import lax
from jax.experimental import pallas as pl
from jax.experimental.pallas import tpu as pltpu

F32 = jnp.float32
BF16 = jnp.bfloat16

EPS = 1e-6
LANES = 128
SUBLANES = 8
VMEM_LIMIT = 56 * 1024 * 1024

HGRN_CHUNK = 128
HGRN_SUB = 16
CONV_HALO = 32
FFN_HALO = 8


def _cparams(*sem):
    return pltpu.CompilerParams(dimension_semantics=sem, vmem_limit_bytes=VMEM_LIMIT)


def _dot(a, b):
    return jnp.dot(a, b, preferred_element_type=F32)


def _dot_nt(a, b):
    return lax.dot_general(a, b, (((1,), (1,)), ((), ())), preferred_element_type=F32)


def _sigmoid(x):
    return 1.0 / (1.0 + jnp.exp(-x))


def _log_sigmoid(x):
    return jnp.minimum(x, 0.0) - jnp.log(1.0 + jnp.exp(-jnp.abs(x)))


def _split3(x):
    hi = x.astype(BF16)
    r = x - hi.astype(F32)
    mid = r.astype(BF16)
    lo = (r - mid.astype(F32)).astype(BF16)
    return hi, mid, lo


def _dot_exact_left(m, x):
    hi, mid, lo = _split3(x)
    return _dot(m, hi) + _dot(m, mid) + _dot(m, lo)


def _dot_exact_right(x, m):
    hi, mid, lo = _split3(x)
    return _dot(hi, m) + _dot(mid, m) + _dot(lo, m)


def _rms_body(x_ref, g_ref, o_ref):
    x = x_ref[...]
    ms = jnp.mean(x * x, axis=-1, keepdims=True)
    o_ref[...] = (x * lax.rsqrt(ms + EPS) * g_ref[...]).astype(o_ref.dtype)


def rmsnorm(x, g, out_dtype, tm):
    n, d = x.shape
    return pl.pallas_call(
        _rms_body,
        out_shape=jax.ShapeDtypeStruct((n, d), out_dtype),
        grid=(n // tm,),
        in_specs=[pl.BlockSpec((tm, d), lambda i: (i, 0)), pl.BlockSpec((1, d), lambda i: (0, 0))],
        out_specs=pl.BlockSpec((tm, d), lambda i: (i, 0)),
        compiler_params=_cparams("arbitrary"),
        name="rmsnorm",
    )(x, g.reshape(1, d))


def _mm_body(*refs, has_res):
    if has_res:
        x_ref, w_ref, r_ref, o_ref, wb_ref = refs
    else:
        x_ref, w_ref, o_ref, wb_ref = refs

    @pl.when(pl.program_id(1) == 0)
    def _():
        wb_ref[...] = w_ref[...].astype(BF16)

    acc = _dot(x_ref[...].astype(BF16), wb_ref[...])
    if has_res:
        acc = acc + r_ref[...]
    o_ref[...] = acc.astype(o_ref.dtype)


def matmul(x, w, col0, ncols, *, tm, tn, out_dtype=F32, residual=None):
    n, k = x.shape
    assert w.shape[0] == k and col0 % tn == 0 and ncols % tn == 0 and n % tm == 0
    c0 = col0 // tn
    in_specs = [pl.BlockSpec((tm, k), lambda j, i: (i, 0)),
                pl.BlockSpec((k, tn), lambda j, i: (0, c0 + j))]
    args = [x, w]
    if residual is not None:
        in_specs.append(pl.BlockSpec((tm, tn), lambda j, i: (i, j)))
        args.append(residual)
    return pl.pallas_call(
        functools.partial(_mm_body, has_res=residual is not None),
        out_shape=jax.ShapeDtypeStruct((n, ncols), out_dtype),
        grid=(ncols // tn, n // tm),
        in_specs=in_specs,
        out_specs=pl.BlockSpec((tm, tn), lambda j, i: (i, j)),
        scratch_shapes=[pltpu.VMEM((k, tn), BF16)],
        compiler_params=_cparams("arbitrary", "arbitrary"),
        name="matmul",
    )(*args)


def _merge_body(oa_ref, ob_ref, oc_ref, wa_ref, wb_ref, wc_ref, ga_ref, gb_ref, gc_ref, o_ref,
                wa_s, wb_s, wc_s):
    @pl.when(pl.program_id(1) == 0)
    def _():
        wa_s[...] = wa_ref[...].astype(BF16)
        wb_s[...] = wb_ref[...].astype(BF16)
        wc_s[...] = wc_ref[...].astype(BF16)

    m = _sigmoid(ga_ref[...]) * _dot(oa_ref[...].astype(BF16), wa_s[...])
    m = m + _sigmoid(gb_ref[...]) * _dot(ob_ref[...].astype(BF16), wb_s[...])
    m = m + _sigmoid(gc_ref[...]) * _dot(oc_ref[...].astype(BF16), wc_s[...])
    o_ref[...] = m.astype(o_ref.dtype)


def branch_merge(oa, ob, oc, wa, wb, wc, gate, *, tm, tn):
    n, d = oa.shape
    nb = d // tn
    act = pl.BlockSpec((tm, d), lambda j, i: (i, 0))
    wsp = pl.BlockSpec((d, tn), lambda j, i: (0, j))
    gsp = [pl.BlockSpec((tm, tn), lambda j, i, b=b: (i, b * nb + j)) for b in range(3)]
    return pl.pallas_call(
        _merge_body,
        out_shape=jax.ShapeDtypeStruct((n, d), BF16),
        grid=(nb, n // tm),
        in_specs=[act, act, act, wsp, wsp, wsp] + gsp,
        out_specs=pl.BlockSpec((tm, tn), lambda j, i: (i, j)),
        scratch_shapes=[pltpu.VMEM((d, tn), BF16)] * 3,
        compiler_params=_cparams("arbitrary", "arbitrary"),
        name="branch_merge",
    )(oa, ob, oc, wa, wb, wc, gate, gate, gate)


def _lb_body(x_ref, o_ref):
    x = x_ref[...]
    depth = x.shape[0]
    mx = jnp.max(x, axis=0, keepdims=True)
    e = jnp.exp(x - mx)
    soft = e / jnp.sum(e, axis=0, keepdims=True)
    run = jnp.zeros_like(soft[0:1])
    o_ref[0:1, :] = run
    for l in range(1, depth):
        run = run + soft[l:l + 1]
        o_ref[l:l + 1, :] = run


def lower_bounds(logits):
    return pl.pallas_call(
        _lb_body, out_shape=jax.ShapeDtypeStruct(logits.shape, F32), name="hgrn_lower_bounds",
    )(logits)


def _logf_body(h_ref, w_ref, b_ref, lf_ref, c_ref, carry):
    t = h_ref.shape[0]
    rows = max(t, LANES)

    @pl.when(pl.program_id(1) == 0)
    def _():
        carry[...] = jnp.zeros_like(carry)

    z = _dot(h_ref[...].astype(BF16), w_ref[...].astype(BF16)) + b_ref[...]
    lf = _log_sigmoid(z)
    lf_ref[...] = lf
    if rows > t:
        lf = jnp.concatenate([lf, jnp.zeros((rows - t, LANES), F32)], axis=0)
    r = lax.broadcasted_iota(jnp.int32, (rows, rows), 0)
    s = lax.broadcasted_iota(jnp.int32, (rows, rows), 1)
    tril = jnp.where(s <= r, 1.0, 0.0).astype(BF16)
    c = _dot_exact_left(tril, lf) + carry[...]
    c_ref[...] = c[:t]
    carry[...] = c[rows - 1:rows]


def fox_logf(h, w_in, col_block, bias_pad, b, t, tt):
    n, k = h.shape
    nt = t // tt
    return pl.pallas_call(
        _logf_body,
        out_shape=(jax.ShapeDtypeStruct((n, LANES), F32), jax.ShapeDtypeStruct((n, LANES), F32)),
        grid=(b, nt),
        in_specs=[pl.BlockSpec((tt, k), lambda i, j: (i * nt + j, 0)),
                  pl.BlockSpec((k, LANES), lambda i, j: (0, col_block)),
                  pl.BlockSpec((1, LANES), lambda i, j: (0, 0))],
        out_specs=(pl.BlockSpec((tt, LANES), lambda i, j: (i * nt + j, 0)),
                   pl.BlockSpec((tt, LANES), lambda i, j: (i * nt + j, 0))),
        scratch_shapes=[pltpu.VMEM((1, LANES), F32)],
        compiler_params=_cparams("arbitrary", "arbitrary"),
        name="fox_logf",
    )(h, w_in, bias_pad)


def _hgrn_body(aq_ref, af_ref, ai_ref, ag_ref, lb_ref, ng_ref, s0_ref, o_ref, s_ref,
               st, g_s, q_s, k_s, v_s, od_s):
    L, C = HGRN_CHUNK, HGRN_SUB
    tb = aq_ref.shape[0]
    c = pl.program_id(2)

    @pl.when(c == 0)
    def _():
        st[...] = s0_ref[0, 0].T

    lb = lb_ref[...]
    qr = aq_ref[...]
    q = qr * _sigmoid(qr)
    fg = lb + (1.0 - lb) * _sigmoid(af_ref[...])
    kk = 1.0 - fg
    gl = jnp.log(fg)
    v = ai_ref[...]
    if tb < L:
        pad = jnp.zeros((L - tb, LANES), F32)
        q, kk, gl, v = (jnp.concatenate([a, pad], axis=0) for a in (q, kk, gl, v))

    r = lax.broadcasted_iota(jnp.int32, (L, L), 0)
    s = lax.broadcasted_iota(jnp.int32, (L, L), 1)
    tril = jnp.where(s <= r, 1.0, 0.0).astype(BF16)
    G = _dot_exact_left(tril, gl)
    st_old = st[...]

    o = _dot_nt((q * jnp.exp(G)).astype(BF16), st_old.astype(BF16))

    row = lax.broadcasted_iota(jnp.int32, (L, 1), 0)
    a_off = jnp.zeros((L, L), F32)
    blk = C
    while blk < L:
        grp = 2 * blk
        gref = jnp.concatenate(
            [jnp.broadcast_to(G[g0 + blk - 1:g0 + blk, :], (grp, LANES)) for g0 in range(0, L, grp)], axis=0)
        upper = (row // blk) % 2 == 1
        qt = q * jnp.exp(jnp.where(upper, G - gref, -jnp.inf))
        kt = kk * jnp.exp(jnp.where(upper, -jnp.inf, gref - G))
        a = _dot_nt(qt.astype(BF16), kt.astype(BF16))
        a_off = a_off + jnp.where(r // grp == s // grp, a, 0.0)
        blk = grp
    o = o + _dot(a_off.astype(BF16), v.astype(BF16))

    g_s[...] = G
    q_s[...] = q
    k_s[...] = kk
    v_s[...] = v
    ones = jnp.ones((LANES, LANES), BF16)
    rowc = lax.broadcasted_iota(jnp.int32, (C, LANES), 0)

    def sub_block(j, carry):
        base = pl.multiple_of(j * C, C)
        gj = g_s[pl.ds(base, C), :]
        qj = q_s[pl.ds(base, C), :]
        parts = []
        for si in range(C):
            gs = g_s[pl.ds(base + si, 1), :]
            ks = k_s[pl.ds(base + si, 1), :]
            e = jnp.exp(jnp.where(rowc >= si, gj - gs, -jnp.inf))
            parts.append(qj * e * ks)
        rs = _dot(jnp.concatenate(parts, axis=0).astype(BF16), ones)
        acc = jnp.zeros((C, LANES), F32)
        for si in range(C):
            acc = acc + rs[si * C:(si + 1) * C] * v_s[pl.ds(base + si, 1), :]
        od_s[pl.ds(base, C), :] = acc
        return carry

    lax.fori_loop(0, L // C, sub_block, 0)
    o = o + od_s[...]

    gl_last = G[L - 1:L, :]
    k_dec = kk * jnp.exp(gl_last - G)
    st_new = st_old * jnp.exp(gl_last) + _dot(v.T.astype(BF16), k_dec.astype(BF16))
    st[...] = st_new

    o = o[:tb]
    ms = jnp.mean(o * o, axis=-1, keepdims=True)
    o = o * lax.rsqrt(ms + EPS) * ng_ref[...] * _sigmoid(ag_ref[...])
    o_ref[...] = o.astype(o_ref.dtype)

    @pl.when(c == pl.num_programs(2) - 1)
    def _():
        s_ref[0, 0] = st_new.T


def hgrn2(za, lb, norm_g, s0, b, t, out_dtype):
    n = za.shape[0]
    heads = s0.shape[1]
    tb = min(t, HGRN_CHUNK)
    nc = t // tb
    col = lambda part: pl.BlockSpec((tb, LANES), lambda i, h, c, part=part: (i * nc + c, part * heads + h))
    return pl.pallas_call(
        _hgrn_body,
        out_shape=(jax.ShapeDtypeStruct((n, heads * LANES), out_dtype),
                   jax.ShapeDtypeStruct(s0.shape, F32)),
        grid=(b, heads, nc),
        in_specs=[col(0), col(1), col(2), col(3),
                  pl.BlockSpec((1, LANES), lambda i, h, c: (0, h)),
                  pl.BlockSpec((1, LANES), lambda i, h, c: (0, 0)),
                  pl.BlockSpec((1, 1, LANES, LANES), lambda i, h, c: (i, h, 0, 0))],
        out_specs=(pl.BlockSpec((tb, LANES), lambda i, h, c: (i * nc + c, h)),
                   pl.BlockSpec((1, 1, LANES, LANES), lambda i, h, c: (i, h, 0, 0))),
        scratch_shapes=[pltpu.VMEM((LANES, LANES), F32)] + [pltpu.VMEM((HGRN_CHUNK, LANES), F32)] * 5,
        compiler_params=_cparams("arbitrary", "arbitrary", "arbitrary"),
        name="hgrn2",
    )(za, za, za, za, lb.reshape(1, -1), norm_g.reshape(1, -1), s0)


def _conv_body(x1_ref, x2_ref, buf_ref, w_ref, b_ref, lg_ref, lbias_ref, o_ref, new_ref, u_s, y_s):
    tt, ch = x1_ref.shape
    width = w_ref.shape[0]
    hist = width - 1
    i = pl.program_id(1)

    @pl.when(i == 0)
    def _():
        u_s[0:CONV_HALO - hist, :] = jnp.zeros((CONV_HALO - hist, ch), F32)
        u_s[CONV_HALO - hist:CONV_HALO, :] = buf_ref[0]

    u_s[CONV_HALO:CONV_HALO + tt, :] = x1_ref[...] * _sigmoid(x2_ref[...])

    def lane_block(cb, carry):
        ls = pl.ds(pl.multiple_of(cb * LANES, LANES), LANES)
        acc = jnp.broadcast_to(b_ref[:, ls], (tt, LANES))
        for j in range(width):
            acc = acc + w_ref[j:j + 1, ls] * u_s[pl.ds(CONV_HALO - hist + j, tt), ls]
        y_s[:, ls] = acc
        return carry

    lax.fori_loop(0, ch // LANES, lane_block, 0)

    y = y_s[...]
    mu = jnp.mean(y, axis=-1, keepdims=True)
    yc = y - mu
    var = jnp.mean(yc * yc, axis=-1, keepdims=True)
    yn = yc * lax.rsqrt(var + EPS) * lg_ref[...] + lbias_ref[...]
    o_ref[...] = (yn * _sigmoid(yn)).astype(o_ref.dtype)

    @pl.when(i == pl.num_programs(1) - 1)
    def _():
        new_ref[0] = u_s[CONV_HALO + tt - hist:CONV_HALO + tt, :]

    if tt >= CONV_HALO:
        @pl.when(i < pl.num_programs(1) - 1)
        def _():
            u_s[0:CONV_HALO, :] = u_s[tt:tt + CONV_HALO, :]


def conv_branch(zb, buf, w, bias, ln_g, ln_b, b, t, tt, out_dtype):
    n = zb.shape[0]
    ch = zb.shape[1] // 2
    width = w.shape[0]
    nt = t // tt
    assert nt == 1 or tt >= CONV_HALO
    vec = pl.BlockSpec((1, ch), lambda i, j: (0, 0))
    return pl.pallas_call(
        _conv_body,
        out_shape=(jax.ShapeDtypeStruct((n, ch), out_dtype),
                   jax.ShapeDtypeStruct((b, width - 1, ch), F32)),
        grid=(b, nt),
        in_specs=[pl.BlockSpec((tt, ch), lambda i, j: (i * nt + j, 0)),
                  pl.BlockSpec((tt, ch), lambda i, j: (i * nt + j, 1)),
                  pl.BlockSpec((1, width - 1, ch), lambda i, j: (i, 0, 0)),
                  pl.BlockSpec((width, ch), lambda i, j: (0, 0)),
                  vec, vec, vec],
        out_specs=(pl.BlockSpec((tt, ch), lambda i, j: (i * nt + j, 0)),
                   pl.BlockSpec((1, width - 1, ch), lambda i, j: (i, 0, 0))),
        scratch_shapes=[pltpu.VMEM((CONV_HALO + tt, ch), F32), pltpu.VMEM((tt, ch), F32)],
        compiler_params=_cparams("arbitrary", "arbitrary"),
        name="conv_branch",
    )(zb, zb, buf, w, bias.reshape(1, ch), ln_g.reshape(1, ch), ln_b.reshape(1, ch))


def _fox_prompt_body(q_ref, k_ref, v_ref, c_ref, ct_ref, o_ref, *, tk):
    tq, dh = q_ref.shape
    scale = dh ** -0.5
    h = pl.program_id(1)
    i = pl.program_id(2)
    q = q_ref[...].astype(BF16)
    lane = lax.broadcasted_iota(jnp.int32, (tq, LANES), 1)
    cq = jnp.sum(jnp.where(lane == h, c_ref[...], 0.0), axis=1, keepdims=True)
    qpos = i * tq + lax.broadcasted_iota(jnp.int32, (tq, tk), 0)
    kidx = lax.broadcasted_iota(jnp.int32, (tq, tk), 1)

    def kv_block(j, carry):
        m, l, acc = carry
        start = pl.multiple_of(j * tk, tk)
        ks = k_ref[pl.ds(start, tk), :].astype(BF16)
        vs = v_ref[pl.ds(start, tk), :].astype(BF16)
        ck = ct_ref[0, pl.ds(h, 1), pl.ds(start, tk)]
        s = _dot_nt(q, ks) * scale + cq - ck
        s = jnp.where(kidx + j * tk <= qpos, s, -jnp.inf)
        m_new = jnp.maximum(m, jnp.max(s, axis=1, keepdims=True))
        p = jnp.exp(s - m_new)
        alpha = jnp.exp(m - m_new)
        l = alpha * l + jnp.sum(p, axis=1, keepdims=True)
        acc = alpha * acc + _dot(p.astype(BF16), vs)
        return m_new, l, acc

    init = (jnp.full((tq, 1), -jnp.inf, F32), jnp.zeros((tq, 1), F32), jnp.zeros((tq, dh), F32))
    _, l, acc = lax.fori_loop(0, (i * tq + tq + tk - 1) // tk, kv_block, init)
    o_ref[...] = (acc / l).astype(o_ref.dtype)


def fox_prompt(cq, ck, cv, c_pad, c_t, b, t, heads, tq, tk):
    n = cq.shape[0]
    dh = cq.shape[1] // heads
    nq = t // tq
    return pl.pallas_call(
        functools.partial(_fox_prompt_body, tk=tk),
        out_shape=jax.ShapeDtypeStruct((n, heads * dh), BF16),
        grid=(b, heads, nq),
        in_specs=[pl.BlockSpec((tq, dh), lambda bi, h, i: (bi * nq + i, h)),
                  pl.BlockSpec((t, dh), lambda bi, h, i: (bi, h)),
                  pl.BlockSpec((t, dh), lambda bi, h, i: (bi, h)),
                  pl.BlockSpec((tq, LANES), lambda bi, h, i: (bi * nq + i, 0)),
                  pl.BlockSpec((1, c_t.shape[1], t), lambda bi, h, i: (bi, 0, 0))],
        out_specs=pl.BlockSpec((tq, dh), lambda bi, h, i: (bi * nq + i, h)),
        compiler_params=_cparams("arbitrary", "arbitrary", "arbitrary"),
        name="fox_prompt",
    )(cq, ck, cv, c_pad, c_t)


def _fox_sample_body(pt_ref, q_ref, kn_ref, vn_ref, cn_ref, cnt_ref, kp_ref, vp_ref, lft_ref, o_ref,
                     m_s, l_s, acc_s, carry_s, *, heads):
    t, d = q_ref.shape
    dh = d // heads
    rows = kp_ref.shape[1]
    scale = dh ** -0.5
    p = pl.program_id(1)

    def online(h, s, vals):
        hs = slice(h * dh, (h + 1) * dh)
        m_old = m_s[h]
        m_new = jnp.maximum(m_old, jnp.max(s, axis=1, keepdims=True))
        pe = jnp.exp(s - m_new)
        alpha = jnp.exp(m_old - m_new)
        l_s[h] = alpha * l_s[h] + jnp.sum(pe, axis=1, keepdims=True)
        acc_s[:, hs] = alpha * acc_s[:, hs] + _dot(pe.astype(BF16), vals.astype(BF16))
        m_s[h] = m_new

    @pl.when(p == 0)
    def _():
        carry_s[...] = jnp.zeros_like(carry_s)
        pad = jnp.zeros((LANES - t, dh), F32)
        r = lax.broadcasted_iota(jnp.int32, (t, LANES), 0)
        c = lax.broadcasted_iota(jnp.int32, (t, LANES), 1)
        for h in range(heads):
            hs = slice(h * dh, (h + 1) * dh)
            kn = jnp.concatenate([kn_ref[:, hs], pad], axis=0)
            vn = jnp.concatenate([vn_ref[:, hs], pad], axis=0)
            s = _dot_nt(q_ref[:, hs].astype(BF16), kn.astype(BF16)) * scale
            s = s + cn_ref[:, h:h + 1] - cnt_ref[0, h:h + 1, :]
            s = jnp.where(c <= r, s, -jnp.inf)
            m0 = jnp.max(s, axis=1, keepdims=True)
            pe = jnp.exp(s - m0)
            m_s[h] = jnp.broadcast_to(m0, (t, LANES))
            l_s[h] = jnp.broadcast_to(jnp.sum(pe, axis=1, keepdims=True), (t, LANES))
            acc_s[:, hs] = _dot(pe.astype(BF16), vn.astype(BF16))

    lft = lft_ref[0]
    rr = lax.broadcasted_iota(jnp.int32, (rows, rows), 0)
    cc = lax.broadcasted_iota(jnp.int32, (rows, rows), 1)
    later = jnp.where(rr > cc, 1.0, 0.0).astype(BF16)
    d_t = _dot_exact_right(lft, later) + carry_s[...]
    carry_s[...] = carry_s[...] + jnp.sum(lft, axis=1, keepdims=True)

    for h in range(heads):
        hs = slice(h * dh, (h + 1) * dh)
        s = _dot_nt(q_ref[:, hs].astype(BF16), kp_ref[0, :, hs].astype(BF16)) * scale
        s = s + d_t[h:h + 1, :] + cn_ref[:, h:h + 1]
        online(h, s, vp_ref[0, :, hs])

    @pl.when(p == pl.num_programs(1) - 1)
    def _():
        for h in range(heads):
            hs = slice(h * dh, (h + 1) * dh)
            o_ref[:, hs] = acc_s[:, hs] / l_s[h]


def fox_sample(cq, ck_new, cv_new, c_pad, c_t_pad, cache_k, cache_v, cache_lft, page_table, heads):
    n, d = cq.shape
    b, n_pages = page_table.shape
    t = n // b
    rows = cache_k.shape[1]
    assert rows == LANES and c_t_pad.shape == (b, heads, LANES)
    tok = lambda bi, p, pt: (bi, 0)
    page = lambda bi, p, pt: (pt[bi, n_pages - 1 - p], 0, 0)
    grid_spec = pltpu.PrefetchScalarGridSpec(
        num_scalar_prefetch=1,
        grid=(b, n_pages),
        in_specs=[pl.BlockSpec((t, d), tok), pl.BlockSpec((t, d), tok), pl.BlockSpec((t, d), tok),
                  pl.BlockSpec((t, LANES), tok),
                  pl.BlockSpec((1, heads, LANES), lambda bi, p, pt: (bi, 0, 0)),
                  pl.BlockSpec((1, rows, d), page), pl.BlockSpec((1, rows, d), page),
                  pl.BlockSpec((1, heads, rows), page)],
        out_specs=pl.BlockSpec((t, d), tok),
        scratch_shapes=[pltpu.VMEM((heads, t, LANES), F32), pltpu.VMEM((heads, t, LANES), F32),
                        pltpu.VMEM((t, d), F32), pltpu.VMEM((heads, LANES), F32)],
    )
    return pl.pallas_call(
        functools.partial(_fox_sample_body, heads=heads),
        out_shape=jax.ShapeDtypeStruct((n, d), F32),
        grid_spec=grid_spec,
        compiler_params=_cparams("arbitrary", "arbitrary"),
        name="fox_sample",
    )(page_table, cq, ck_new, cv_new, c_pad, c_t_pad, cache_k, cache_v, cache_lft)


def _mem_attn_body(q_ref, k_ref, v_ref, o_ref, *, heads):
    d = q_ref.shape[1]
    dh = d // heads
    scale = dh ** -0.5
    for h in range(heads):
        hs = slice(h * dh, (h + 1) * dh)
        s = _dot_nt(q_ref[:, hs].astype(BF16), k_ref[0, :, hs].astype(BF16)) * scale
        m = jnp.max(s, axis=1, keepdims=True)
        pe = jnp.exp(s - m)
        o = _dot(pe.astype(BF16), v_ref[0, :, hs].astype(BF16)) / jnp.sum(pe, axis=1, keepdims=True)
        o_ref[:, hs] = o.astype(o_ref.dtype)


def mem_attention(qm, mem_k, mem_v, b, t, tt, heads, out_dtype):
    n, d = qm.shape
    nm = mem_k.shape[1]
    nt = t // tt
    return pl.pallas_call(
        functools.partial(_mem_attn_body, heads=heads),
        out_shape=jax.ShapeDtypeStruct((n, d), out_dtype),
        grid=(b, nt),
        in_specs=[pl.BlockSpec((tt, d), lambda i, j: (i * nt + j, 0)),
                  pl.BlockSpec((1, nm, d), lambda i, j: (i, 0, 0)),
                  pl.BlockSpec((1, nm, d), lambda i, j: (i, 0, 0))],
        out_specs=pl.BlockSpec((tt, d), lambda i, j: (i * nt + j, 0)),
        compiler_params=_cparams("arbitrary", "arbitrary"),
        name="mem_attention",
    )(qm, mem_k, mem_v)


def _ffn_act_body(g_ref, v_ref, buf_ref, w_ref, b_ref, o_ref, new_ref, g_s):
    t, tc = g_ref.shape
    width = w_ref.shape[0]
    hist = width - 1
    g_s[FFN_HALO - hist:FFN_HALO, :] = buf_ref[0]
    g_s[FFN_HALO:FFN_HALO + t, :] = g_ref[...]
    acc = jnp.broadcast_to(b_ref[...], (t, tc))
    for j in range(width):
        acc = acc + w_ref[j:j + 1, :] * g_s[pl.ds(FFN_HALO - hist + j, t), :]
    o_ref[...] = (acc * _sigmoid(acc) * v_ref[...]).astype(o_ref.dtype)
    new_ref[0] = g_s[FFN_HALO + t - hist:FFN_HALO + t, :]


def ffn_act(up, buf, w, bias, b, t, tc, out_dtype):
    n = up.shape[0]
    dff = up.shape[1] // 2
    width = w.shape[0]
    nj = dff // tc
    return pl.pallas_call(
        _ffn_act_body,
        out_shape=(jax.ShapeDtypeStruct((n, dff), out_dtype),
                   jax.ShapeDtypeStruct((b, width - 1, dff), F32)),
        grid=(b, nj),
        in_specs=[pl.BlockSpec((t, tc), lambda i, j: (i, j)),
                  pl.BlockSpec((t, tc), lambda i, j: (i, nj + j)),
                  pl.BlockSpec((1, width - 1, tc), lambda i, j: (i, 0, j)),
                  pl.BlockSpec((width, tc), lambda i, j: (0, j)),
                  pl.BlockSpec((1, tc), lambda i, j: (0, j))],
        out_specs=(pl.BlockSpec((t, tc), lambda i, j: (i, j)),
                   pl.BlockSpec((1, width - 1, tc), lambda i, j: (i, 0, j))),
        scratch_shapes=[pltpu.VMEM((FFN_HALO + t, tc), F32)],
        compiler_params=_cparams("arbitrary", "arbitrary"),
        name="ffn_act",
    )(up, up, buf, w, bias.reshape(1, dff))


def _trunk_layer(x, b, t, p, lb, mem_k, mem_v, mem_heads, s0, conv_buf, ffn_buf, attend, prompt):
    n, d = x.shape
    heads_a = s0.shape[1]
    wa_cols = heads_a * LANES
    act = BF16 if prompt else F32
    tm = min(n, 1024)
    tn = 512
    w_in = p['w_in']
    off_b = 4 * wa_cols
    off_c = off_b + 2 * d
    off_f = off_c + 3 * d
    fox_heads = p['fox_f_bias'].shape[0]
    off_g = off_f + fox_heads

    h1 = rmsnorm(x, p['norm_mix_g'], act, min(n, 512))
    za = matmul(h1, w_in, 0, off_b, tm=tm, tn=tn)
    zb = matmul(h1, w_in, off_b, 2 * d, tm=tm, tn=tn)
    cq = matmul(h1, w_in, off_c, d, tm=tm, tn=tn)
    ck = matmul(h1, w_in, off_c + d, d, tm=tm, tn=tn)
    cv = matmul(h1, w_in, off_c + 2 * d, d, tm=tm, tn=tn)
    gate = matmul(h1, w_in[:, off_g:], 0, 3 * d, tm=tm, tn=tn)
    bias_pad = jnp.zeros((1, LANES), F32).at[0, :fox_heads].set(p['fox_f_bias'])
    logf_pad, c_pad = fox_logf(h1, w_in, off_f // LANES, bias_pad, b, t, min(t, 256))

    oa, s_new = hgrn2(za, lb, p['hgrn_norm_g'], s0, b, t, act)
    ob, conv_new = conv_branch(zb, conv_buf, p['conv_dw_w'], p['conv_dw_b'], p['conv_ln_g'], p['conv_ln_b'],
                               b, t, min(t, 256), act)
    c_t = c_pad[:, :fox_heads].reshape(b, t, fox_heads).transpose(0, 2, 1)
    oc = attend(cq, ck, cv, c_pad, c_t)

    m = branch_merge(oa, ob, oc, p['w_branch_a'], p['w_branch_b'], p['w_branch_c'], gate,
                     tm=min(n, 512), tn=256)
    x = matmul(m, p['w_out'], 0, d, tm=tm, tn=tn, residual=x)

    h2 = rmsnorm(x, p['norm_mem_g'], act, min(n, 512))
    qm = matmul(h2, p['w_mq'], 0, d, tm=tm, tn=tn, out_dtype=act)
    om = mem_attention(qm, mem_k, mem_v, b, t, min(t, 512), mem_heads, act)
    x = matmul(om, p['w_mo'], 0, d, tm=tm, tn=tn, residual=x)

    h3 = rmsnorm(x, p['norm_ffn_g'], act, min(n, 512))
    dff = p['w_down'].shape[0]
    up = matmul(h3, p['w_up'], 0, 2 * dff, tm=tm, tn=tn)
    a, ffn_new = ffn_act(up, ffn_buf, p['ffn_dw_w'], p['ffn_dw_b'], b, t, 512, act)
    x = matmul(a, p['w_down'], 0, d, tm=min(n, 512), tn=256, residual=x)

    logf = logf_pad[:, :fox_heads].reshape(b, t, fox_heads)
    return x, (ck, cv, logf, s_new, conv_new, ffn_new)


def kernel(x_prompt, x_sample, cache_k, cache_v, cache_logf, page_table, cache_mem_k, cache_mem_v, state_hgrn, state_conv, state_ffn, mem_prompt, norm_mix_g, w_in, hgrn_lb_logits, hgrn_norm_g, conv_dw_w, conv_dw_b, conv_ln_g, conv_ln_b, fox_f_bias, w_branch_a, w_branch_b, w_branch_c, w_out, norm_mem_g, mem_kv_norm_g, w_mq, w_mk, w_mv, w_mo, norm_ffn_g, w_up, ffn_dw_w, ffn_dw_b, w_down, final_norm_g):
    depth = w_in.shape[0]
    bp, tp, d = x_prompt.shape
    bs, ts, _ = x_sample.shape
    n_mem = mem_prompt.shape[1]
    fox_heads = fox_f_bias.shape[1]
    dh = d // fox_heads
    hgrn_heads = state_hgrn.shape[2]
    n_pool, page_rows = cache_k.shape[1], cache_k.shape[2]
    mem_heads = cache_mem_k.shape[3]

    lbs = lower_bounds(hgrn_lb_logits.astype(F32))
    xp = x_prompt.reshape(bp * tp, d)
    xs = x_sample.reshape(bs * ts, d)
    memp = mem_prompt.reshape(bp * n_mem, d)
    rows_p, rows_s, mem_rows = [], [], []
    for l in range(depth):
        p = {'norm_mix_g': norm_mix_g[l], 'w_in': w_in[l], 'hgrn_norm_g': hgrn_norm_g[l],
             'conv_dw_w': conv_dw_w[l], 'conv_dw_b': conv_dw_b[l], 'conv_ln_g': conv_ln_g[l],
             'conv_ln_b': conv_ln_b[l], 'fox_f_bias': fox_f_bias[l], 'w_branch_a': w_branch_a[l],
             'w_branch_b': w_branch_b[l], 'w_branch_c': w_branch_c[l], 'w_out': w_out[l],
             'norm_mem_g': norm_mem_g[l], 'w_mq': w_mq[l], 'w_mo': w_mo[l], 'norm_ffn_g': norm_ffn_g[l],
             'w_up': w_up[l], 'ffn_dw_w': ffn_dw_w[l], 'ffn_dw_b': ffn_dw_b[l], 'w_down': w_down[l]}
        lb = lbs[l]

        mn = rmsnorm(memp, mem_kv_norm_g[l], BF16, 512)
        mk = matmul(mn, w_mk[l], 0, d, tm=bp * n_mem, tn=512)
        mv = matmul(mn, w_mv[l], 0, d, tm=bp * n_mem, tn=512)
        mem_rows.append((mk, mv))
        s0 = jnp.zeros((bp, hgrn_heads, LANES, LANES), F32)
        cb0 = jnp.zeros((bp, conv_dw_w.shape[1] - 1, d), F32)
        fb0 = jnp.zeros((bp, ffn_dw_w.shape[1] - 1, w_down.shape[1]), F32)
        attend_p = lambda cq, ck, cv, c_pad, c_t: fox_prompt(cq, ck, cv, c_pad, c_t, bp, tp, fox_heads, 256, 256)
        xp, new_p = _trunk_layer(xp, bp, tp, p, lb, mk.reshape(bp, n_mem, d), mv.reshape(bp, n_mem, d),
                                 mem_heads, s0, cb0, fb0, attend_p, True)
        rows_p.append(new_p)

        ck_pages = cache_k[l].reshape(n_pool, page_rows, d)
        cv_pages = cache_v[l].reshape(n_pool, page_rows, d)
        lft_pages = cache_logf[l].astype(F32).transpose(0, 2, 1)

        def attend_s(cq, ck, cv, c_pad, c_t):
            c_t_pad = jnp.pad(c_t, ((0, 0), (0, 0), (0, LANES - ts)))
            return fox_sample(cq, ck, cv, c_pad, c_t_pad, ck_pages, cv_pages, lft_pages, page_table, fox_heads)

        xs, new_s = _trunk_layer(xs, bs, ts, p, lb, cache_mem_k[l].reshape(bs, n_mem, d),
                                 cache_mem_v[l].reshape(bs, n_mem, d), mem_heads, state_hgrn[l],
                                 state_conv[l], state_ffn[l], attend_s, False)
        rows_s.append(new_s)

    y_prompt = rmsnorm(xp, final_norm_g, F32, 512).reshape(bp, tp, d)
    y_sample = rmsnorm(xs, final_norm_g, F32, bs * ts).reshape(bs, ts, d)

    def stack(rows, i, shape):
        return jnp.stack([r[i].reshape(shape) for r in rows])

    k_prompt = stack(rows_p, 0, (bp, tp, fox_heads, dh))
    v_prompt = stack(rows_p, 1, (bp, tp, fox_heads, dh))
    logf_prompt = stack(rows_p, 2, (bp, tp, fox_heads))
    hgrn_prompt = stack(rows_p, 3, (bp, hgrn_heads, LANES, LANES))
    conv_prompt = stack(rows_p, 4, (bp, conv_dw_w.shape[1] - 1, d))
    ffn_prompt = stack(rows_p, 5, (bp, ffn_dw_w.shape[1] - 1, w_down.shape[1]))
    mem_k_prompt = stack(mem_rows, 0, (bp, n_mem, mem_heads, d // mem_heads))
    mem_v_prompt = stack(mem_rows, 1, (bp, n_mem, mem_heads, d // mem_heads))
    k_sample = stack(rows_s, 0, (bs, ts, fox_heads, dh))
    v_sample = stack(rows_s, 1, (bs, ts, fox_heads, dh))
    logf_sample = stack(rows_s, 2, (bs, ts, fox_heads))
    hgrn_sample = stack(rows_s, 3, (bs, hgrn_heads, LANES, LANES))
    conv_sample = stack(rows_s, 4, (bs, conv_dw_w.shape[1] - 1, d))
    ffn_sample = stack(rows_s, 5, (bs, ffn_dw_w.shape[1] - 1, w_down.shape[1]))
    return (y_prompt, y_sample, k_prompt, v_prompt, logf_prompt, mem_k_prompt, mem_v_prompt, hgrn_prompt,
            conv_prompt, ffn_prompt, k_sample, v_sample, logf_sample, hgrn_sample, conv_sample, ffn_sample)
```

```python
import functools

import jax
import jax.numpy as jnp
from jax import lax
from jax.experimental import pallas as pl
from jax.experimental.pallas import tpu as pltpu

F32 = jnp.float32
BF16 = jnp.bfloat16

EPS = 1e-6
LANES = 128
SUBLANES = 8
VMEM_LIMIT = 56 * 1024 * 1024

HGRN_CHUNK = 128
HGRN_SUB = 16
HGRN_HEAD_BLOCK = 4
CONV_HALO = 32
FFN_HALO = 8
FOX_PAGES_PER_STEP = 2


def _cparams(*sem):
    return pltpu.CompilerParams(dimension_semantics=sem, vmem_limit_bytes=VMEM_LIMIT)


def _dot(a, b):
    return jnp.dot(a, b, preferred_element_type=F32)


def _dot_nt(a, b):
    return lax.dot_general(a, b, (((1,), (1,)), ((), ())), preferred_element_type=F32)


def _sigmoid(x):
    return 1.0 / (1.0 + jnp.exp(-x))


def _log_sigmoid(x):
    return jnp.minimum(x, 0.0) - jnp.log(1.0 + jnp.exp(-jnp.abs(x)))


def _split3(x):
    hi = x.astype(BF16)
    r = x - hi.astype(F32)
    mid = r.astype(BF16)
    lo = (r - mid.astype(F32)).astype(BF16)
    return hi, mid, lo


def _dot_exact_left(m, x):
    hi, mid, lo = _split3(x)
    return _dot(m, hi) + _dot(m, mid) + _dot(m, lo)


def _dot_exact_right(x, m):
    hi, mid, lo = _split3(x)
    return _dot(hi, m) + _dot(mid, m) + _dot(lo, m)


def _rms_body(x_ref, g_ref, o_ref):
    x = x_ref[...]
    ms = jnp.mean(x * x, axis=-1, keepdims=True)
    o_ref[...] = (x * lax.rsqrt(ms + EPS) * g_ref[...]).astype(o_ref.dtype)


def rmsnorm(x, g, out_dtype, tm):
    n, d = x.shape
    return pl.pallas_call(
        _rms_body,
        out_shape=jax.ShapeDtypeStruct((n, d), out_dtype),
        grid=(n // tm,),
        in_specs=[pl.BlockSpec((tm, d), lambda i: (i, 0)), pl.BlockSpec((1, d), lambda i: (0, 0))],
        out_specs=pl.BlockSpec((tm, d), lambda i: (i, 0)),
        compiler_params=_cparams("arbitrary"),
        name="rmsnorm",
    )(x, g.reshape(1, d))


def _mm_body(*refs, has_res):
    if has_res:
        x_ref, w_ref, r_ref, o_ref, wb_ref = refs
    else:
        x_ref, w_ref, o_ref, wb_ref = refs

    @pl.when(pl.program_id(1) == 0)
    def _():
        wb_ref[...] = w_ref[...].astype(BF16)

    acc = _dot(x_ref[...].astype(BF16), wb_ref[...])
    if has_res:
        acc = acc + r_ref[...]
    o_ref[...] = acc.astype(o_ref.dtype)


def matmul(x, w, layer, col0, ncols, *, tm, tn, out_dtype=F32, residual=None):
    n, k = x.shape
    assert w.shape[1] == k and col0 % tn == 0 and ncols % tn == 0 and n % tm == 0
    c0 = col0 // tn
    in_specs = [pl.BlockSpec((tm, k), lambda j, i: (i, 0)),
                pl.BlockSpec((None, k, tn), lambda j, i: (layer, 0, c0 + j))]
    args = [x, w]
    if residual is not None:
        in_specs.append(pl.BlockSpec((tm, tn), lambda j, i: (i, j)))
        args.append(residual)
    return pl.pallas_call(
        functools.partial(_mm_body, has_res=residual is not None),
        out_shape=jax.ShapeDtypeStruct((n, ncols), out_dtype),
        grid=(ncols // tn, n // tm),
        in_specs=in_specs,
        out_specs=pl.BlockSpec((tm, tn), lambda j, i: (i, j)),
        scratch_shapes=[pltpu.VMEM((k, tn), BF16)],
        compiler_params=_cparams("arbitrary", "arbitrary"),
        name="matmul",
    )(*args)


def _merge_body(oa_ref, ob_ref, oc_ref, wa_ref, wb_ref, wc_ref, ga_ref, gb_ref, gc_ref, o_ref,
                wa_s, wb_s, wc_s):
    @pl.when(pl.program_id(1) == 0)
    def _():
        wa_s[...] = wa_ref[...].astype(BF16)
        wb_s[...] = wb_ref[...].astype(BF16)
        wc_s[...] = wc_ref[...].astype(BF16)

    m = _sigmoid(ga_ref[...]) * _dot(oa_ref[...].astype(BF16), wa_s[...])
    m = m + _sigmoid(gb_ref[...]) * _dot(ob_ref[...].astype(BF16), wb_s[...])
    m = m + _sigmoid(gc_ref[...]) * _dot(oc_ref[...].astype(BF16), wc_s[...])
    o_ref[...] = m.astype(o_ref.dtype)


def branch_merge(oa, ob, oc, wa, wb, wc, layer, gate, *, tm, tn):
    n, d = oa.shape
    nb = d // tn
    act = pl.BlockSpec((tm, d), lambda j, i: (i, 0))
    wsp = pl.BlockSpec((None, d, tn), lambda j, i: (layer, 0, j))
    gsp = [pl.BlockSpec((tm, tn), lambda j, i, b=b: (i, b * nb + j)) for b in range(3)]
    return pl.pallas_call(
        _merge_body,
        out_shape=jax.ShapeDtypeStruct((n, d), BF16),
        grid=(nb, n // tm),
        in_specs=[act, act, act, wsp, wsp, wsp] + gsp,
        out_specs=pl.BlockSpec((tm, tn), lambda j, i: (i, j)),
        scratch_shapes=[pltpu.VMEM((d, tn), BF16)] * 3,
        compiler_params=_cparams("arbitrary", "arbitrary"),
        name="branch_merge",
    )(oa, ob, oc, wa, wb, wc, gate, gate, gate)


def _lb_body(x_ref, o_ref):
    x = x_ref[...]
    depth = x.shape[0]
    mx = jnp.max(x, axis=0, keepdims=True)
    e = jnp.exp(x - mx)
    soft = e / jnp.sum(e, axis=0, keepdims=True)
    run = jnp.zeros_like(soft[0:1])
    o_ref[0:1, :] = run
    for l in range(1, depth):
        run = run + soft[l:l + 1]
        o_ref[l:l + 1, :] = run


def lower_bounds(logits):
    return pl.pallas_call(
        _lb_body, out_shape=jax.ShapeDtypeStruct(logits.shape, F32), name="hgrn_lower_bounds",
    )(logits)


def _logf_body(h_ref, w_ref, b_ref, lf_ref, c_ref, carry):
    t = h_ref.shape[0]
    rows = max(t, LANES)

    @pl.when(pl.program_id(1) == 0)
    def _():
        carry[...] = jnp.zeros_like(carry)

    z = _dot(h_ref[...].astype(BF16), w_ref[...].astype(BF16)) + b_ref[...]
    lf = _log_sigmoid(z)
    lf_ref[...] = lf
    if rows > t:
        lf = jnp.concatenate([lf, jnp.zeros((rows - t, LANES), F32)], axis=0)
    r = lax.broadcasted_iota(jnp.int32, (rows, rows), 0)
    s = lax.broadcasted_iota(jnp.int32, (rows, rows), 1)
    tril = jnp.where(s <= r, 1.0, 0.0).astype(BF16)
    c = _dot_exact_left(tril, lf) + carry[...]
    c_ref[...] = c[:t]
    carry[...] = c[rows - 1:rows]


def fox_logf(h, w_in, layer, col_block, bias_pad, b, t, tt):
    n, k = h.shape
    nt = t // tt
    return pl.pallas_call(
        _logf_body,
        out_shape=(jax.ShapeDtypeStruct((n, LANES), F32), jax.ShapeDtypeStruct((n, LANES), F32)),
        grid=(b, nt),
        in_specs=[pl.BlockSpec((tt, k), lambda i, j: (i * nt + j, 0)),
                  pl.BlockSpec((None, k, LANES), lambda i, j: (layer, 0, col_block)),
                  pl.BlockSpec((1, LANES), lambda i, j: (0, 0))],
        out_specs=(pl.BlockSpec((tt, LANES), lambda i, j: (i * nt + j, 0)),
                   pl.BlockSpec((tt, LANES), lambda i, j: (i * nt + j, 0))),
        scratch_shapes=[pltpu.VMEM((1, LANES), F32)],
        compiler_params=_cparams("arbitrary", "arbitrary"),
        name="fox_logf",
    )(h, w_in, bias_pad)


def _hgrn_body(aq_ref, af_ref, ai_ref, ag_ref, lb_ref, ng_ref, s0_ref, o_ref, s_ref,
               st, g_s, q_s, k_s, v_s, o_s, *, hb):
    L, C = HGRN_CHUNK, HGRN_SUB
    tb = aq_ref.shape[0]
    c = pl.program_id(2)

    @pl.when(c == 0)
    def _():
        for h in range(hb):
            st[h] = s0_ref[0, h].T

    r = lax.broadcasted_iota(jnp.int32, (L, L), 0)
    s = lax.broadcasted_iota(jnp.int32, (L, L), 1)
    tril = jnp.where(s <= r, 1.0, 0.0).astype(BF16)
    row = lax.broadcasted_iota(jnp.int32, (L, 1), 0)

    for h in range(hb):
        ls = slice(h * LANES, (h + 1) * LANES)
        lb = lb_ref[:, ls]
        qr = aq_ref[:, ls]
        q = qr * _sigmoid(qr)
        fg = lb + (1.0 - lb) * _sigmoid(af_ref[:, ls])
        kk = 1.0 - fg
        gl = jnp.log(fg)
        v = ai_ref[:, ls]
        if tb < L:
            pad = jnp.zeros((L - tb, LANES), F32)
            q, kk, gl, v = (jnp.concatenate([a, pad], axis=0) for a in (q, kk, gl, v))

        G = _dot_exact_left(tril, gl)
        st_old = st[h]

        o = _dot_nt((q * jnp.exp(G)).astype(BF16), st_old.astype(BF16))

        a_off = jnp.zeros((L, L), F32)
        blk = C
        while blk < L:
            grp = 2 * blk
            gref = jnp.concatenate(
                [jnp.broadcast_to(G[g0 + blk - 1:g0 + blk, :], (grp, LANES)) for g0 in range(0, L, grp)],
                axis=0)
            upper = (row // blk) % 2 == 1
            qt = q * jnp.exp(jnp.where(upper, G - gref, -jnp.inf))
            kt = kk * jnp.exp(jnp.where(upper, -jnp.inf, gref - G))
            a = _dot_nt(qt.astype(BF16), kt.astype(BF16))
            a_off = a_off + jnp.where(r // grp == s // grp, a, 0.0)
            blk = grp
        o_s[h] = o + _dot(a_off.astype(BF16), v.astype(BF16))

        g_s[h] = G
        q_s[h] = q
        k_s[h] = kk
        v_s[h] = v

        gl_last = G[L - 1:L, :]
        k_dec = kk * jnp.exp(gl_last - G)
        st[h] = st_old * jnp.exp(gl_last) + _dot(v.T.astype(BF16), k_dec.astype(BF16))

    ones = jnp.ones((LANES, LANES), BF16)
    rowc = lax.broadcasted_iota(jnp.int32, (C, LANES), 0)

    def sub_block(j, carry):
        base = pl.multiple_of(j * C, C)
        for h in range(hb):
            gj = g_s[h, pl.ds(base, C), :]
            qj = q_s[h, pl.ds(base, C), :]
            parts = []
            for si in range(C):
                gs = g_s[h, pl.ds(base + si, 1), :]
                ks = k_s[h, pl.ds(base + si, 1), :]
                e = jnp.exp(jnp.where(rowc >= si, gj - gs, -jnp.inf))
                parts.append(qj * e * ks)
            rs = _dot(jnp.concatenate(parts, axis=0).astype(BF16), ones)
            acc = o_s[h, pl.ds(base, C), :]
            for si in range(C):
                acc = acc + rs[si * C:(si + 1) * C] * v_s[h, pl.ds(base + si, 1), :]
            o_s[h, pl.ds(base, C), :] = acc
        return carry

    lax.fori_loop(0, L // C, sub_block, 0)

    for h in range(hb):
        ls = slice(h * LANES, (h + 1) * LANES)
        o = o_s[h][:tb]
        ms = jnp.mean(o * o, axis=-1, keepdims=True)
        o = o * lax.rsqrt(ms + EPS) * ng_ref[...] * _sigmoid(ag_ref[:, ls])
        o_ref[:, ls] = o.astype(o_ref.dtype)

    @pl.when(c == pl.num_programs(2) - 1)
    def _():
        for h in range(hb):
            s_ref[0, h] = st[h].T


def hgrn2(za, lb, norm_g, s0, b, t, out_dtype):
    n = za.shape[0]
    heads = s0.shape[1]
    assert s0.shape[2:] == (LANES, LANES)
    hb = min(HGRN_HEAD_BLOCK, heads)
    nhb = heads // hb
    tb = min(t, HGRN_CHUNK)
    nc = t // tb
    w = hb * LANES
    col = lambda part: pl.BlockSpec((tb, w), lambda i, h, c, part=part: (i * nc + c, part * nhb + h))
    return pl.pallas_call(
        functools.partial(_hgrn_body, hb=hb),
        out_shape=(jax.ShapeDtypeStruct((n, heads * LANES), out_dtype),
                   jax.ShapeDtypeStruct(s0.shape, F32)),
        grid=(b, nhb, nc),
        in_specs=[col(0), col(1), col(2), col(3),
                  pl.BlockSpec((1, w), lambda i, h, c: (0, h)),
                  pl.BlockSpec((1, LANES), lambda i, h, c: (0, 0)),
                  pl.BlockSpec((1, hb, LANES, LANES), lambda i, h, c: (i, h, 0, 0))],
        out_specs=(pl.BlockSpec((tb, w), lambda i, h, c: (i * nc + c, h)),
                   pl.BlockSpec((1, hb, LANES, LANES), lambda i, h, c: (i, h, 0, 0))),
        scratch_shapes=[pltpu.VMEM((hb, LANES, LANES), F32)] + [pltpu.VMEM((hb, HGRN_CHUNK, LANES), F32)] * 5,
        compiler_params=_cparams("arbitrary", "arbitrary", "arbitrary"),
        name="hgrn2",
    )(za, za, za, za, lb.reshape(1, -1), norm_g.reshape(1, -1), s0)


def _conv_body(x1_ref, x2_ref, buf_ref, w_ref, b_ref, lg_ref, lbias_ref, o_ref, new_ref, u_s, y_s):
    tt, ch = x1_ref.shape
    width = w_ref.shape[0]
    hist = width - 1
    i = pl.program_id(1)

    @pl.when(i == 0)
    def _():
        u_s[0:CONV_HALO - hist, :] = jnp.zeros((CONV_HALO - hist, ch), F32)
        u_s[CONV_HALO - hist:CONV_HALO, :] = buf_ref[0]

    u_s[CONV_HALO:CONV_HALO + tt, :] = x1_ref[...] * _sigmoid(x2_ref[...])

    def lane_block(cb, carry):
        ls = pl.ds(pl.multiple_of(cb * LANES, LANES), LANES)
        acc = jnp.broadcast_to(b_ref[:, ls], (tt, LANES))
        for j in range(width):
            acc = acc + w_ref[j:j + 1, ls] * u_s[pl.ds(CONV_HALO - hist + j, tt), ls]
        y_s[:, ls] = acc
        return carry

    lax.fori_loop(0, ch // LANES, lane_block, 0)

    y = y_s[...]
    mu = jnp.mean(y, axis=-1, keepdims=True)
    yc = y - mu
    var = jnp.mean(yc * yc, axis=-1, keepdims=True)
    yn = yc * lax.rsqrt(var + EPS) * lg_ref[...] + lbias_ref[...]
    o_ref[...] = (yn * _sigmoid(yn)).astype(o_ref.dtype)

    @pl.when(i == pl.num_programs(1) - 1)
    def _():
        new_ref[0] = u_s[CONV_HALO + tt - hist:CONV_HALO + tt, :]

    if tt >= CONV_HALO:
        @pl.when(i < pl.num_programs(1) - 1)
        def _():
            u_s[0:CONV_HALO, :] = u_s[tt:tt + CONV_HALO, :]


def conv_branch(zb, buf, w, bias, ln_g, ln_b, b, t, tt, out_dtype):
    n = zb.shape[0]
    ch = zb.shape[1] // 2
    width = w.shape[0]
    nt = t // tt
    assert nt == 1 or tt >= CONV_HALO
    vec = pl.BlockSpec((1, ch), lambda i, j: (0, 0))
    return pl.pallas_call(
        _conv_body,
        out_shape=(jax.ShapeDtypeStruct((n, ch), out_dtype),
                   jax.ShapeDtypeStruct((b, width - 1, ch), F32)),
        grid=(b, nt),
        in_specs=[pl.BlockSpec((tt, ch), lambda i, j: (i * nt + j, 0)),
                  pl.BlockSpec((tt, ch), lambda i, j: (i * nt + j, 1)),
                  pl.BlockSpec((1, width - 1, ch), lambda i, j: (i, 0, 0)),
                  pl.BlockSpec((width, ch), lambda i, j: (0, 0)),
                  vec, vec, vec],
        out_specs=(pl.BlockSpec((tt, ch), lambda i, j: (i * nt + j, 0)),
                   pl.BlockSpec((1, width - 1, ch), lambda i, j: (i, 0, 0))),
        scratch_shapes=[pltpu.VMEM((CONV_HALO + tt, ch), F32), pltpu.VMEM((tt, ch), F32)],
        compiler_params=_cparams("arbitrary", "arbitrary"),
        name="conv_branch",
    )(zb, zb, buf, w, bias.reshape(1, ch), ln_g.reshape(1, ch), ln_b.reshape(1, ch))


def _fox_prompt_body(q_ref, k_ref, v_ref, c_ref, ct_ref, o_ref, *, tk):
    tq, dh = q_ref.shape
    scale = dh ** -0.5
    h = pl.program_id(1)
    i = pl.program_id(2)
    q = q_ref[...].astype(BF16)
    lane = lax.broadcasted_iota(jnp.int32, (tq, LANES), 1)
    cq = jnp.sum(jnp.where(lane == h, c_ref[...], 0.0), axis=1, keepdims=True)
    qpos = i * tq + lax.broadcasted_iota(jnp.int32, (tq, tk), 0)
    kidx = lax.broadcasted_iota(jnp.int32, (tq, tk), 1)

    def kv_block(j, carry):
        m, l, acc = carry
        start = pl.multiple_of(j * tk, tk)
        ks = k_ref[pl.ds(start, tk), :].astype(BF16)
        vs = v_ref[pl.ds(start, tk), :].astype(BF16)
        ck = ct_ref[0, pl.ds(h, 1), pl.ds(start, tk)]
        s = _dot_nt(q, ks) * scale + cq - ck
        s = jnp.where(kidx + j * tk <= qpos, s, -jnp.inf)
        m_new = jnp.maximum(m, jnp.max(s, axis=1, keepdims=True))
        p = jnp.exp(s - m_new)
        alpha = jnp.exp(m - m_new)
        l = alpha * l + jnp.sum(p, axis=1, keepdims=True)
        acc = alpha * acc + _dot(p.astype(BF16), vs)
        return m_new, l, acc

    init = (jnp.full((tq, 1), -jnp.inf, F32), jnp.zeros((tq, 1), F32), jnp.zeros((tq, dh), F32))
    _, l, acc = lax.fori_loop(0, (i * tq + tq + tk - 1) // tk, kv_block, init)
    o_ref[...] = (acc / l).astype(o_ref.dtype)


def fox_prompt(cq, ck, cv, c_pad, c_t, b, t, heads, tq, tk):
    n = cq.shape[0]
    dh = cq.shape[1] // heads
    nq = t // tq
    return pl.pallas_call(
        functools.partial(_fox_prompt_body, tk=tk),
        out_shape=jax.ShapeDtypeStruct((n, heads * dh), BF16),
        grid=(b, heads, nq),
        in_specs=[pl.BlockSpec((tq, dh), lambda bi, h, i: (bi * nq + i, h)),
                  pl.BlockSpec((t, dh), lambda bi, h, i: (bi, h)),
                  pl.BlockSpec((t, dh), lambda bi, h, i: (bi, h)),
                  pl.BlockSpec((tq, LANES), lambda bi, h, i: (bi * nq + i, 0)),
                  pl.BlockSpec((1, c_t.shape[1], t), lambda bi, h, i: (bi, 0, 0))],
        out_specs=pl.BlockSpec((tq, dh), lambda bi, h, i: (bi * nq + i, h)),
        compiler_params=_cparams("arbitrary", "arbitrary", "arbitrary"),
        name="fox_prompt",
    )(cq, ck, cv, c_pad, c_t)


def _fox_sample_body(pt_ref, q_ref, kn_ref, vn_ref, ccol_ref, crow_ref, *rest, heads, gp):
    kp_refs, vp_refs, lft_refs = rest[:gp], rest[gp:2 * gp], rest[2 * gp:3 * gp]
    o_ref, m_s, l_s, acc_s, carry_s = rest[3 * gp:]
    t, d = q_ref.shape
    dh = d // heads
    rows = lft_refs[0].shape[1]
    scale = dh ** -0.5
    p = pl.program_id(1)
    qb = [q_ref[:, h * dh:(h + 1) * dh].astype(BF16) for h in range(heads)]
    ccol = ccol_ref[0]

    def update(s, values_of):
        m_old = m_s[...]
        m_new = jnp.maximum(m_old, jnp.max(s, axis=1, keepdims=True))
        pe = jnp.exp(s - m_new)
        alpha = jnp.exp(m_old - m_new)
        l_s[...] = alpha * l_s[...] + jnp.sum(pe, axis=1, keepdims=True)
        pv = [_dot(pe[h * t:(h + 1) * t].astype(BF16), values_of(h)) for h in range(heads)]
        acc_s[...] = alpha * acc_s[...] + jnp.concatenate(pv, axis=0)
        m_s[...] = m_new

    @pl.when(p == 0)
    def _():
        carry_s[...] = jnp.zeros_like(carry_s)
        m_s[...] = jnp.full_like(m_s, -jnp.inf)
        l_s[...] = jnp.zeros_like(l_s)
        acc_s[...] = jnp.zeros_like(acc_s)
        pad = jnp.zeros((rows - t, dh), F32)
        sc = []
        for h in range(heads):
            kn = jnp.concatenate([kn_ref[:, h * dh:(h + 1) * dh], pad], axis=0)
            sc.append(_dot_nt(qb[h], kn.astype(BF16)) * scale)
        s = jnp.concatenate(sc, axis=0) + ccol - crow_ref[0]
        r = lax.broadcasted_iota(jnp.int32, (heads * t, rows), 0)
        c = lax.broadcasted_iota(jnp.int32, (heads * t, rows), 1)
        s = jnp.where(c <= r % t, s, -jnp.inf)
        update(s, lambda h: jnp.concatenate([vn_ref[:, h * dh:(h + 1) * dh], pad], axis=0).astype(BF16))

    rr = lax.broadcasted_iota(jnp.int32, (rows, rows), 0)
    cc = lax.broadcasted_iota(jnp.int32, (rows, rows), 1)
    later = jnp.where(rr > cc, 1.0, 0.0).astype(BF16)
    for g in range(gp):
        kp_ref, vp_ref = kp_refs[g], vp_refs[g]
        lft = lft_refs[g][...]
        d_t = _dot_exact_right(lft, later) + carry_s[...]
        carry_s[...] = carry_s[...] + jnp.sum(lft, axis=1, keepdims=True)
        sc = []
        for h in range(heads):
            kh = kp_ref[pl.ds(h, rows, stride=heads), :].astype(BF16)
            sc.append(_dot_nt(qb[h], kh) * scale + d_t[h:h + 1, :])
        s = jnp.concatenate(sc, axis=0) + ccol
        update(s, lambda h: vp_ref[pl.ds(h, rows, stride=heads), :].astype(BF16))

    @pl.when(p == pl.num_programs(1) - 1)
    def _():
        o = acc_s[...] / l_s[...]
        for h in range(heads):
            o_ref[:, h * dh:(h + 1) * dh] = o[h * t:(h + 1) * t]


def fox_sample(cq, ck_new, cv_new, c_col, c_row, cache_k, cache_v, cache_lft, layer, page_table, heads):
    n, d = cq.shape
    b, n_pages = page_table.shape
    t = n // b
    dh = d // heads
    rows = cache_lft.shape[3]
    gp = FOX_PAGES_PER_STEP
    assert rows == LANES and dh == LANES and n_pages % gp == 0
    tok = lambda bi, p, pt: (bi, 0)
    tab = lambda bi, p, pt: (bi, 0, 0)

    def page(g):
        return lambda bi, p, pt: (layer, pt[bi, n_pages - 1 - (gp * p + g)], 0, 0)

    kv_specs = [pl.BlockSpec((None, None, rows * heads, dh), page(g)) for g in range(gp)]
    lf_specs = [pl.BlockSpec((None, None, heads, rows), page(g)) for g in range(gp)]
    grid_spec = pltpu.PrefetchScalarGridSpec(
        num_scalar_prefetch=1,
        grid=(b, n_pages // gp),
        in_specs=[pl.BlockSpec((t, d), tok), pl.BlockSpec((t, d), tok), pl.BlockSpec((t, d), tok),
                  pl.BlockSpec((1, heads * t, LANES), tab), pl.BlockSpec((1, heads * t, LANES), tab)]
        + kv_specs + kv_specs + lf_specs,
        out_specs=pl.BlockSpec((t, d), tok),
        scratch_shapes=[pltpu.VMEM((heads * t, LANES), F32), pltpu.VMEM((heads * t, LANES), F32),
                        pltpu.VMEM((heads * t, dh), F32), pltpu.VMEM((heads, LANES), F32)],
    )
    return pl.pallas_call(
        functools.partial(_fox_sample_body, heads=heads, gp=gp),
        out_shape=jax.ShapeDtypeStruct((n, d), F32),
        grid_spec=grid_spec,
        compiler_params=_cparams("arbitrary", "arbitrary"),
        name="fox_sample",
    )(page_table, cq, ck_new, cv_new, c_col, c_row, *([cache_k] * gp), *([cache_v] * gp), *([cache_lft] * gp))


def _mem_attn_body(q_ref, k_ref, v_ref, o_ref, *, heads):
    d = q_ref.shape[1]
    dh = d // heads
    scale = dh ** -0.5
    for h in range(heads):
        hs = slice(h * dh, (h + 1) * dh)
        s = _dot_nt(q_ref[:, hs].astype(BF16), k_ref[0, :, hs].astype(BF16)) * scale
        m = jnp.max(s, axis=1, keepdims=True)
        pe = jnp.exp(s - m)
        o = _dot(pe.astype(BF16), v_ref[0, :, hs].astype(BF16)) / jnp.sum(pe, axis=1, keepdims=True)
        o_ref[:, hs] = o.astype(o_ref.dtype)


def mem_attention(qm, mem_k, mem_v, b, t, tt, heads, out_dtype):
    n, d = qm.shape
    nm = mem_k.shape[1]
    nt = t // tt
    return pl.pallas_call(
        functools.partial(_mem_attn_body, heads=heads),
        out_shape=jax.ShapeDtypeStruct((n, d), out_dtype),
        grid=(b, nt),
        in_specs=[pl.BlockSpec((tt, d), lambda i, j: (i * nt + j, 0)),
                  pl.BlockSpec((1, nm, d), lambda i, j: (i, 0, 0)),
                  pl.BlockSpec((1, nm, d), lambda i, j: (i, 0, 0))],
        out_specs=pl.BlockSpec((tt, d), lambda i, j: (i * nt + j, 0)),
        compiler_params=_cparams("arbitrary", "arbitrary"),
        name="mem_attention",
    )(qm, mem_k, mem_v)


def _ffn_act_body(g_ref, v_ref, buf_ref, w_ref, b_ref, o_ref, new_ref, g_s):
    t, tc = g_ref.shape
    width = w_ref.shape[0]
    hist = width - 1
    g_s[FFN_HALO - hist:FFN_HALO, :] = buf_ref[0]
    g_s[FFN_HALO:FFN_HALO + t, :] = g_ref[...]
    acc = jnp.broadcast_to(b_ref[...], (t, tc))
    for j in range(width):
        acc = acc + w_ref[j:j + 1, :] * g_s[pl.ds(FFN_HALO - hist + j, t), :]
    o_ref[...] = (acc * _sigmoid(acc) * v_ref[...]).astype(o_ref.dtype)
    new_ref[0] = g_s[FFN_HALO + t - hist:FFN_HALO + t, :]


def ffn_act(up, buf, w, bias, b, t, tc, out_dtype):
    n = up.shape[0]
    dff = up.shape[1] // 2
    width = w.shape[0]
    nj = dff // tc
    return pl.pallas_call(
        _ffn_act_body,
        out_shape=(jax.ShapeDtypeStruct((n, dff), out_dtype),
                   jax.ShapeDtypeStruct((b, width - 1, dff), F32)),
        grid=(b, nj),
        in_specs=[pl.BlockSpec((t, tc), lambda i, j: (i, j)),
                  pl.BlockSpec((t, tc), lambda i, j: (i, nj + j)),
                  pl.BlockSpec((1, width - 1, tc), lambda i, j: (i, 0, j)),
                  pl.BlockSpec((width, tc), lambda i, j: (0, j)),
                  pl.BlockSpec((1, tc), lambda i, j: (0, j))],
        out_specs=(pl.BlockSpec((t, tc), lambda i, j: (i, j)),
                   pl.BlockSpec((1, width - 1, tc), lambda i, j: (i, 0, j))),
        scratch_shapes=[pltpu.VMEM((FFN_HALO + t, tc), F32)],
        compiler_params=_cparams("arbitrary", "arbitrary"),
        name="ffn_act",
    )(up, up, buf, w, bias.reshape(1, dff))


def _trunk_layer(x, b, t, l, w, p, lb, mem_k, mem_v, mem_heads, s0, conv_buf, ffn_buf, attend, prompt):
    n, d = x.shape
    heads_a = s0.shape[1]
    wa_cols = heads_a * LANES
    act = BF16 if prompt else F32
    tm = min(n, 1024)
    tn = 512
    w_in = w['w_in']
    off_b = 4 * wa_cols
    off_c = off_b + 2 * d
    off_f = off_c + 3 * d
    fox_heads = p['fox_f_bias'].shape[0]

    h1 = rmsnorm(x, p['norm_mix_g'], act, min(n, 512))
    za = matmul(h1, w_in, l, 0, off_b, tm=tm, tn=tn)
    zb = matmul(h1, w_in, l, off_b, 2 * d, tm=tm, tn=tn)
    cq = matmul(h1, w_in, l, off_c, d, tm=tm, tn=tn)
    ck = matmul(h1, w_in, l, off_c + d, d, tm=tm, tn=tn)
    cv = matmul(h1, w_in, l, off_c + 2 * d, d, tm=tm, tn=tn)
    gate = matmul(h1, w['w_gate'], l, 0, 3 * d, tm=tm, tn=tn)
    bias_pad = jnp.zeros((1, LANES), F32).at[0, :fox_heads].set(p['fox_f_bias'])
    logf_pad, c_pad = fox_logf(h1, w_in, l, off_f // LANES, bias_pad, b, t, min(t, 256))

    oa, s_new = hgrn2(za, lb, p['hgrn_norm_g'], s0, b, t, act)
    ob, conv_new = conv_branch(zb, conv_buf, p['conv_dw_w'], p['conv_dw_b'], p['conv_ln_g'], p['conv_ln_b'],
                               b, t, min(t, 256), act)
    c_t = c_pad[:, :fox_heads].reshape(b, t, fox_heads).transpose(0, 2, 1)
    oc = attend(cq, ck, cv, c_pad, c_t)

    m = branch_merge(oa, ob, oc, w['w_branch_a'], w['w_branch_b'], w['w_branch_c'], l, gate,
                     tm=min(n, 512), tn=256)
    x = matmul(m, w['w_out'], l, 0, d, tm=tm, tn=tn, residual=x)

    h2 = rmsnorm(x, p['norm_mem_g'], act, min(n, 512))
    qm = matmul(h2, w['w_mq'], l, 0, d, tm=tm, tn=tn, out_dtype=act)
    om = mem_attention(qm, mem_k, mem_v, b, t, min(t, 512), mem_heads, act)
    x = matmul(om, w['w_mo'], l, 0, d, tm=tm, tn=tn, residual=x)

    h3 = rmsnorm(x, p['norm_ffn_g'], act, min(n, 512))
    dff = w['w_down'].shape[1]
    up = matmul(h3, w['w_up'], l, 0, 2 * dff, tm=tm, tn=tn)
    a, ffn_new = ffn_act(up, ffn_buf, p['ffn_dw_w'], p['ffn_dw_b'], b, t, 512, act)
    x = matmul(a, w['w_down'], l, 0, d, tm=min(n, 512), tn=256, residual=x)

    logf = logf_pad[:, :fox_heads].reshape(b, t, fox_heads)
    return x, (ck, cv, logf, s_new, conv_new, ffn_new)


def kernel(x_prompt, x_sample, cache_k, cache_v, cache_logf, page_table, cache_mem_k, cache_mem_v, state_hgrn, state_conv, state_ffn, mem_prompt, norm_mix_g, w_in, hgrn_lb_logits, hgrn_norm_g, conv_dw_w, conv_dw_b, conv_ln_g, conv_ln_b, fox_f_bias, w_branch_a, w_branch_b, w_branch_c, w_out, norm_mem_g, mem_kv_norm_g, w_mq, w_mk, w_mv, w_mo, norm_ffn_g, w_up, ffn_dw_w, ffn_dw_b, w_down, final_norm_g):
    depth = w_in.shape[0]
    bp, tp, d = x_prompt.shape
    bs, ts, _ = x_sample.shape
    n_mem = mem_prompt.shape[1]
    fox_heads = fox_f_bias.shape[1]
    dh = d // fox_heads
    hgrn_heads = state_hgrn.shape[2]
    n_pool, page_rows = cache_k.shape[1], cache_k.shape[2]
    mem_heads = cache_mem_k.shape[3]

    gate_off = w_in.shape[2] - 3 * d
    w = {'w_in': w_in, 'w_gate': w_in[:, :, gate_off:], 'w_branch_a': w_branch_a, 'w_branch_b': w_branch_b,
         'w_branch_c': w_branch_c, 'w_out': w_out, 'w_mq': w_mq, 'w_mo': w_mo, 'w_up': w_up, 'w_down': w_down}

    ck_pages = cache_k.reshape(depth, n_pool, page_rows * fox_heads, dh)
    cv_pages = cache_v.reshape(depth, n_pool, page_rows * fox_heads, dh)
    lft_pages = cache_logf.astype(F32).transpose(0, 1, 3, 2)

    lbs = lower_bounds(hgrn_lb_logits.astype(F32))
    xp = x_prompt.reshape(bp * tp, d)
    xs = x_sample.reshape(bs * ts, d)
    memp = mem_prompt.reshape(bp * n_mem, d)
    rows_p, rows_s, mem_rows = [], [], []
    for l in range(depth):
        p = {'norm_mix_g': norm_mix_g[l], 'hgrn_norm_g': hgrn_norm_g[l],
             'conv_dw_w': conv_dw_w[l], 'conv_dw_b': conv_dw_b[l], 'conv_ln_g': conv_ln_g[l],
             'conv_ln_b': conv_ln_b[l], 'fox_f_bias': fox_f_bias[l],
             'norm_mem_g': norm_mem_g[l], 'norm_ffn_g': norm_ffn_g[l],
             'ffn_dw_w': ffn_dw_w[l], 'ffn_dw_b': ffn_dw_b[l]}
        lb = lbs[l]

        mn = rmsnorm(memp, mem_kv_norm_g[l], BF16, 512)
        mk = matmul(mn, w_mk, l, 0, d, tm=bp * n_mem, tn=512)
        mv = matmul(mn, w_mv, l, 0, d, tm=bp * n_mem, tn=512)
        mem_rows.append((mk, mv))
        s0 = jnp.zeros((bp, hgrn_heads, LANES, LANES), F32)
        cb0 = jnp.zeros((bp, conv_dw_w.shape[1] - 1, d), F32)
        fb0 = jnp.zeros((bp, ffn_dw_w.shape[1] - 1, w_down.shape[1]), F32)
        attend_p = lambda cq, ck, cv, c_pad, c_t: fox_prompt(cq, ck, cv, c_pad, c_t, bp, tp, fox_heads, 256, 256)
        xp, new_p = _trunk_layer(xp, bp, tp, l, w, p, lb, mk.reshape(bp, n_mem, d), mv.reshape(bp, n_mem, d),
                                 mem_heads, s0, cb0, fb0, attend_p, True)
        rows_p.append(new_p)

        def attend_s(cq, ck, cv, c_pad, c_t, l=l):
            c_col = jnp.broadcast_to(c_t.reshape(bs, fox_heads * ts, 1), (bs, fox_heads * ts, LANES))
            c_row = jnp.repeat(jnp.pad(c_t, ((0, 0), (0, 0), (0, LANES - ts))), ts, axis=1)
            return fox_sample(cq, ck, cv, c_col, c_row, ck_pages, cv_pages, lft_pages, l, page_table, fox_heads)

        xs, new_s = _trunk_layer(xs, bs, ts, l, w, p, lb, cache_mem_k[l].reshape(bs, n_mem, d),
                                 cache_mem_v[l].reshape(bs, n_mem, d), mem_heads, state_hgrn[l],
                                 state_conv[l], state_ffn[l], attend_s, False)
        rows_s.append(new_s)

    y_prompt = rmsnorm(xp, final_norm_g, F32, 512).reshape(bp, tp, d)
    y_sample = rmsnorm(xs, final_norm_g, F32, bs * ts).reshape(bs, ts, d)

    def stack(rows, i, shape):
        return jnp.stack([r[i].reshape(shape) for r in rows])

    k_prompt = stack(rows_p, 0, (bp, tp, fox_heads, dh))
    v_prompt = stack(rows_p, 1, (bp, tp, fox_heads, dh))
    logf_prompt = stack(rows_p, 2, (bp, tp, fox_heads))
    hgrn_prompt = stack(rows_p, 3, (bp, hgrn_heads, LANES, LANES))
    conv_prompt = stack(rows_p, 4, (bp, conv_dw_w.shape[1] - 1, d))
    ffn_prompt = stack(rows_p, 5, (bp, ffn_dw_w.shape[1] - 1, w_down.shape[1]))
    mem_k_prompt = stack(mem_rows, 0, (bp, n_mem, mem_heads, d // mem_heads))
    mem_v_prompt = stack(mem_rows, 1, (bp, n_mem, mem_heads, d // mem_heads))
    k_sample = stack(rows_s, 0, (bs, ts, fox_heads, dh))
    v_sample = stack(rows_s, 1, (bs, ts, fox_heads, dh))
    logf_sample = stack(rows_s, 2, (bs, ts, fox_heads))
    hgrn_sample = stack(rows_s, 3, (bs, hgrn_heads, LANES, LANES))
    conv_sample = stack(rows_s, 4, (bs, conv_dw_w.shape[1] - 1, d))
    ffn_sample = stack(rows_s, 5, (bs, ffn_dw_w.shape[1] - 1, w_down.shape[1]))
    return (y_prompt, y_sample, k_prompt, v_prompt, logf_prompt, mem_k_prompt, mem_v_prompt, hgrn_prompt,
            conv_prompt, ffn_prompt, k_sample, v_sample, logf_sample, hgrn_sample, conv_sample, ffn_sample)
```

```python
import functools

import jax
import jax.numpy as jnp
from jax import lax
from jax.experimental import pallas as pl
from jax.experimental.pallas import tpu as pltpu

F32 = jnp.float32
BF16 = jnp.bfloat16

EPS = 1e-6
LANES = 128
SUBLANES = 8
VMEM_LIMIT = 56 * 1024 * 1024

HGRN_CHUNK = 128
HGRN_SUB = 16
HGRN_HEAD_BLOCK = 4
CONV_HALO = 32
FFN_HALO = 8
FOX_PAGES_PER_STEP = 2


def _cparams(*sem):
    return pltpu.CompilerParams(dimension_semantics=sem, vmem_limit_bytes=VMEM_LIMIT)


def _dot(a, b):
    return jnp.dot(a, b, preferred_element_type=F32)


def _dot_nt(a, b):
    return lax.dot_general(a, b, (((1,), (1,)), ((), ())), preferred_element_type=F32)


def _sigmoid(x):
    return 1.0 / (1.0 + jnp.exp(-x))


def _log_sigmoid(x):
    return jnp.minimum(x, 0.0) - jnp.log(1.0 + jnp.exp(-jnp.abs(x)))


def _split3(x):
    hi = x.astype(BF16)
    r = x - hi.astype(F32)
    mid = r.astype(BF16)
    lo = (r - mid.astype(F32)).astype(BF16)
    return hi, mid, lo


def _dot_exact_left(m, x):
    hi, mid, lo = _split3(x)
    return _dot(m, hi) + _dot(m, mid) + _dot(m, lo)


def _dot_exact_right(x, m):
    hi, mid, lo = _split3(x)
    return _dot(hi, m) + _dot(mid, m) + _dot(lo, m)


def _rms_body(x_ref, g_ref, o_ref):
    x = x_ref[...]
    ms = jnp.mean(x * x, axis=-1, keepdims=True)
    o_ref[...] = (x * lax.rsqrt(ms + EPS) * g_ref[...]).astype(o_ref.dtype)


def rmsnorm(x, g, out_dtype, tm):
    n, d = x.shape
    return pl.pallas_call(
        _rms_body,
        out_shape=jax.ShapeDtypeStruct((n, d), out_dtype),
        grid=(n // tm,),
        in_specs=[pl.BlockSpec((tm, d), lambda i: (i, 0)), pl.BlockSpec((1, d), lambda i: (0, 0))],
        out_specs=pl.BlockSpec((tm, d), lambda i: (i, 0)),
        compiler_params=_cparams("arbitrary"),
        name="rmsnorm",
    )(x, g.reshape(1, d))


def _mm_body(*refs, has_res, shift):
    refs = list(refs)
    x_ref, w_ref = refs[0], refs[1]
    if shift:
        w2_ref, tail_ref = refs[2:4]
        del refs[2:4]
    if has_res:
        r_ref, o_ref, wb_ref = refs[2:]
    else:
        o_ref, wb_ref = refs[2:]

    @pl.when(pl.program_id(1) == 0)
    def _():
        if shift:
            last = pl.program_id(0) == pl.num_programs(0) - 1
            nxt = jnp.where(last, tail_ref[...], w2_ref[...])
            wide = jnp.concatenate([w_ref[...], nxt], axis=1)
            tn = w_ref.shape[1]
            wb_ref[...] = pltpu.roll(wide, wide.shape[1] - shift, 1)[:, :tn].astype(BF16)
        else:
            wb_ref[...] = w_ref[...].astype(BF16)

    acc = _dot(x_ref[...].astype(BF16), wb_ref[...])
    if has_res:
        acc = acc + r_ref[...]
    o_ref[...] = acc.astype(o_ref.dtype)


def matmul(x, w, layer, col0, ncols, *, tm, tn, out_dtype=F32, residual=None):
    n, k = x.shape
    shift = col0 % LANES
    col0 -= shift
    assert w.shape[1] == k and col0 % tn == 0 and ncols % tn == 0 and n % tm == 0
    c0 = col0 // tn
    in_specs = [pl.BlockSpec((tm, k), lambda j, i: (i, 0)),
                pl.BlockSpec((None, k, tn), lambda j, i: (layer, 0, c0 + j))]
    args = [x, w]
    if shift:
        per = tn // LANES
        nj = ncols // tn
        end = col0 + ncols
        assert end + shift <= w.shape[2]
        tail = jnp.pad(w[layer, :, end:end + shift], ((0, 0), (0, LANES - shift)))
        in_specs.append(pl.BlockSpec((None, k, LANES),
                                     lambda j, i: (layer, 0, (c0 + jnp.minimum(j + 1, nj - 1)) * per)))
        in_specs.append(pl.BlockSpec((k, LANES), lambda j, i: (0, 0)))
        args += [w, tail]
    if residual is not None:
        in_specs.append(pl.BlockSpec((tm, tn), lambda j, i: (i, j)))
        args.append(residual)
    return pl.pallas_call(
        functools.partial(_mm_body, has_res=residual is not None, shift=shift),
        out_shape=jax.ShapeDtypeStruct((n, ncols), out_dtype),
        grid=(ncols // tn, n // tm),
        in_specs=in_specs,
        out_specs=pl.BlockSpec((tm, tn), lambda j, i: (i, j)),
        scratch_shapes=[pltpu.VMEM((k, tn), BF16)],
        compiler_params=_cparams("arbitrary", "arbitrary"),
        name="matmul",
    )(*args)


def _merge_body(oa_ref, ob_ref, oc_ref, wa_ref, wb_ref, wc_ref, ga_ref, gb_ref, gc_ref, o_ref,
                wa_s, wb_s, wc_s):
    @pl.when(pl.program_id(1) == 0)
    def _():
        wa_s[...] = wa_ref[...].astype(BF16)
        wb_s[...] = wb_ref[...].astype(BF16)
        wc_s[...] = wc_ref[...].astype(BF16)

    m = _sigmoid(ga_ref[...]) * _dot(oa_ref[...].astype(BF16), wa_s[...])
    m = m + _sigmoid(gb_ref[...]) * _dot(ob_ref[...].astype(BF16), wb_s[...])
    m = m + _sigmoid(gc_ref[...]) * _dot(oc_ref[...].astype(BF16), wc_s[...])
    o_ref[...] = m.astype(o_ref.dtype)


def branch_merge(oa, ob, oc, wa, wb, wc, layer, gate, *, tm, tn):
    n, d = oa.shape
    nb = d // tn
    act = pl.BlockSpec((tm, d), lambda j, i: (i, 0))
    wsp = pl.BlockSpec((None, d, tn), lambda j, i: (layer, 0, j))
    gsp = [pl.BlockSpec((tm, tn), lambda j, i, b=b: (i, b * nb + j)) for b in range(3)]
    return pl.pallas_call(
        _merge_body,
        out_shape=jax.ShapeDtypeStruct((n, d), BF16),
        grid=(nb, n // tm),
        in_specs=[act, act, act, wsp, wsp, wsp] + gsp,
        out_specs=pl.BlockSpec((tm, tn), lambda j, i: (i, j)),
        scratch_shapes=[pltpu.VMEM((d, tn), BF16)] * 3,
        compiler_params=_cparams("arbitrary", "arbitrary"),
        name="branch_merge",
    )(oa, ob, oc, wa, wb, wc, gate, gate, gate)


def _lb_body(x_ref, o_ref):
    x = x_ref[...]
    depth = x.shape[0]
    mx = jnp.max(x, axis=0, keepdims=True)
    e = jnp.exp(x - mx)
    soft = e / jnp.sum(e, axis=0, keepdims=True)
    run = jnp.zeros_like(soft[0:1])
    o_ref[0:1, :] = run
    for l in range(1, depth):
        run = run + soft[l:l + 1]
        o_ref[l:l + 1, :] = run


def lower_bounds(logits):
    return pl.pallas_call(
        _lb_body, out_shape=jax.ShapeDtypeStruct(logits.shape, F32), name="hgrn_lower_bounds",
    )(logits)


def _logf_body(h_ref, w_ref, b_ref, lf_ref, c_ref, carry):
    t = h_ref.shape[0]
    rows = max(t, LANES)

    @pl.when(pl.program_id(1) == 0)
    def _():
        carry[...] = jnp.zeros_like(carry)

    z = _dot(h_ref[...].astype(BF16), w_ref[...].astype(BF16)) + b_ref[...]
    lf = _log_sigmoid(z)
    lf_ref[...] = lf
    if rows > t:
        lf = jnp.concatenate([lf, jnp.zeros((rows - t, LANES), F32)], axis=0)
    r = lax.broadcasted_iota(jnp.int32, (rows, rows), 0)
    s = lax.broadcasted_iota(jnp.int32, (rows, rows), 1)
    tril = jnp.where(s <= r, 1.0, 0.0).astype(BF16)
    c = _dot_exact_left(tril, lf) + carry[...]
    c_ref[...] = c[:t]
    carry[...] = c[rows - 1:rows]


def fox_logf(h, w_in, layer, col_block, bias_pad, b, t, tt):
    n, k = h.shape
    nt = t // tt
    return pl.pallas_call(
        _logf_body,
        out_shape=(jax.ShapeDtypeStruct((n, LANES), F32), jax.ShapeDtypeStruct((n, LANES), F32)),
        grid=(b, nt),
        in_specs=[pl.BlockSpec((tt, k), lambda i, j: (i * nt + j, 0)),
                  pl.BlockSpec((None, k, LANES), lambda i, j: (layer, 0, col_block)),
                  pl.BlockSpec((1, LANES), lambda i, j: (0, 0))],
        out_specs=(pl.BlockSpec((tt, LANES), lambda i, j: (i * nt + j, 0)),
                   pl.BlockSpec((tt, LANES), lambda i, j: (i * nt + j, 0))),
        scratch_shapes=[pltpu.VMEM((1, LANES), F32)],
        compiler_params=_cparams("arbitrary", "arbitrary"),
        name="fox_logf",
    )(h, w_in, bias_pad)


def _hgrn_body(aq_ref, af_ref, ai_ref, ag_ref, lb_ref, ng_ref, s0_ref, o_ref, s_ref,
               st, g_s, q_s, k_s, v_s, o_s, *, hb):
    L, C = HGRN_CHUNK, HGRN_SUB
    tb = aq_ref.shape[0]
    c = pl.program_id(2)

    @pl.when(c == 0)
    def _():
        for h in range(hb):
            st[h] = s0_ref[0, h].T

    r = lax.broadcasted_iota(jnp.int32, (L, L), 0)
    s = lax.broadcasted_iota(jnp.int32, (L, L), 1)
    tril = jnp.where(s <= r, 1.0, 0.0).astype(BF16)
    row = lax.broadcasted_iota(jnp.int32, (L, 1), 0)

    for h in range(hb):
        ls = slice(h * LANES, (h + 1) * LANES)
        lb = lb_ref[:, ls]
        qr = aq_ref[:, ls]
        q = qr * _sigmoid(qr)
        fg = lb + (1.0 - lb) * _sigmoid(af_ref[:, ls])
        kk = 1.0 - fg
        gl = jnp.log(fg)
        v = ai_ref[:, ls]
        if tb < L:
            pad = jnp.zeros((L - tb, LANES), F32)
            q, kk, gl, v = (jnp.concatenate([a, pad], axis=0) for a in (q, kk, gl, v))

        G = _dot_exact_left(tril, gl)
        st_old = st[h]

        o = _dot_nt((q * jnp.exp(G)).astype(BF16), st_old.astype(BF16))

        a_off = jnp.zeros((L, L), F32)
        blk = C
        while blk < L:
            grp = 2 * blk
            gref = jnp.concatenate(
                [jnp.broadcast_to(G[g0 + blk - 1:g0 + blk, :], (grp, LANES)) for g0 in range(0, L, grp)],
                axis=0)
            upper = (row // blk) % 2 == 1
            qt = q * jnp.exp(jnp.where(upper, G - gref, -jnp.inf))
            kt = kk * jnp.exp(jnp.where(upper, -jnp.inf, gref - G))
            a = _dot_nt(qt.astype(BF16), kt.astype(BF16))
            a_off = a_off + jnp.where(r // grp == s // grp, a, 0.0)
            blk = grp
        o_s[h] = o + _dot(a_off.astype(BF16), v.astype(BF16))

        g_s[h] = G
        q_s[h] = q
        k_s[h] = kk
        v_s[h] = v

        gl_last = G[L - 1:L, :]
        k_dec = kk * jnp.exp(gl_last - G)
        st[h] = st_old * jnp.exp(gl_last) + _dot(v.T.astype(BF16), k_dec.astype(BF16))

    ones = jnp.ones((LANES, LANES), BF16)
    rowc = lax.broadcasted_iota(jnp.int32, (C, LANES), 0)

    def sub_block(j, carry):
        base = pl.multiple_of(j * C, C)
        for h in range(hb):
            gj = g_s[h, pl.ds(base, C), :]
            qj = q_s[h, pl.ds(base, C), :]
            parts = []
            for si in range(C):
                gs = g_s[h, pl.ds(base + si, 1), :]
                ks = k_s[h, pl.ds(base + si, 1), :]
                e = jnp.exp(jnp.where(rowc >= si, gj - gs, -jnp.inf))
                parts.append(qj * e * ks)
            rs = _dot(jnp.concatenate(parts, axis=0).astype(BF16), ones)
            acc = o_s[h, pl.ds(base, C), :]
            for si in range(C):
                acc = acc + rs[si * C:(si + 1) * C] * v_s[h, pl.ds(base + si, 1), :]
            o_s[h, pl.ds(base, C), :] = acc
        return carry

    lax.fori_loop(0, L // C, sub_block, 0)

    for h in range(hb):
        ls = slice(h * LANES, (h + 1) * LANES)
        o = o_s[h][:tb]
        ms = jnp.mean(o * o, axis=-1, keepdims=True)
        o = o * lax.rsqrt(ms + EPS) * ng_ref[...] * _sigmoid(ag_ref[:, ls])
        o_ref[:, ls] = o.astype(o_ref.dtype)

    @pl.when(c == pl.num_programs(2) - 1)
    def _():
        for h in range(hb):
            s_ref[0, h] = st[h].T


def hgrn2(za, lb, norm_g, s0, b, t, out_dtype):
    n = za.shape[0]
    heads = s0.shape[1]
    assert s0.shape[2:] == (LANES, LANES)
    hb = min(HGRN_HEAD_BLOCK, heads)
    nhb = heads // hb
    tb = min(t, HGRN_CHUNK)
    nc = t // tb
    w = hb * LANES
    col = lambda part: pl.BlockSpec((tb, w), lambda i, h, c, part=part: (i * nc + c, part * nhb + h))
    return pl.pallas_call(
        functools.partial(_hgrn_body, hb=hb),
        out_shape=(jax.ShapeDtypeStruct((n, heads * LANES), out_dtype),
                   jax.ShapeDtypeStruct(s0.shape, F32)),
        grid=(b, nhb, nc),
        in_specs=[col(0), col(1), col(2), col(3),
                  pl.BlockSpec((1, w), lambda i, h, c: (0, h)),
                  pl.BlockSpec((1, LANES), lambda i, h, c: (0, 0)),
                  pl.BlockSpec((1, hb, LANES, LANES), lambda i, h, c: (i, h, 0, 0))],
        out_specs=(pl.BlockSpec((tb, w), lambda i, h, c: (i * nc + c, h)),
                   pl.BlockSpec((1, hb, LANES, LANES), lambda i, h, c: (i, h, 0, 0))),
        scratch_shapes=[pltpu.VMEM((hb, LANES, LANES), F32)] + [pltpu.VMEM((hb, HGRN_CHUNK, LANES), F32)] * 5,
        compiler_params=_cparams("arbitrary", "arbitrary", "arbitrary"),
        name="hgrn2",
    )(za, za, za, za, lb.reshape(1, -1), norm_g.reshape(1, -1), s0)


def _conv_body(x1_ref, x2_ref, buf_ref, w_ref, b_ref, lg_ref, lbias_ref, o_ref, new_ref, u_s, y_s, sh_s):
    tt, ch = x1_ref.shape
    width = w_ref.shape[0]
    hist = width - 1
    i = pl.program_id(1)

    @pl.when(i == 0)
    def _():
        u_s[0:CONV_HALO - hist, :] = jnp.zeros((CONV_HALO - hist, ch), F32)
        u_s[CONV_HALO - hist:CONV_HALO, :] = buf_ref[0]

    u_s[CONV_HALO:CONV_HALO + tt, :] = x1_ref[...] * _sigmoid(x2_ref[...])

    def lane_block(cb, carry):
        ls = pl.ds(pl.multiple_of(cb * LANES, LANES), LANES)
        acc = jnp.broadcast_to(b_ref[:, ls], (tt, LANES))
        for s in range(SUBLANES):
            taps = [j for j in range(width) if (CONV_HALO - hist + j) % SUBLANES == s]
            if not taps:
                continue
            span = max(CONV_HALO - hist + j - s for j in taps) + tt
            if s:
                sh_s[s, 0:span, :] = u_s[pl.ds(s, span), ls]
            for j in taps:
                off = CONV_HALO - hist + j - s
                xs = sh_s[s, off:off + tt, :] if s else u_s[off:off + tt, ls]
                acc = acc + w_ref[j:j + 1, ls] * xs
        y_s[:, ls] = acc
        return carry

    lax.fori_loop(0, ch // LANES, lane_block, 0)

    y = y_s[...]
    mu = jnp.mean(y, axis=-1, keepdims=True)
    yc = y - mu
    var = jnp.mean(yc * yc, axis=-1, keepdims=True)
    yn = yc * lax.rsqrt(var + EPS) * lg_ref[...] + lbias_ref[...]
    o_ref[...] = (yn * _sigmoid(yn)).astype(o_ref.dtype)

    @pl.when(i == pl.num_programs(1) - 1)
    def _():
        new_ref[0] = u_s[CONV_HALO + tt - hist:CONV_HALO + tt, :]

    if tt >= CONV_HALO:
        @pl.when(i < pl.num_programs(1) - 1)
        def _():
            u_s[0:CONV_HALO, :] = u_s[tt:tt + CONV_HALO, :]


def conv_branch(zb, buf, w, bias, ln_g, ln_b, b, t, tt, out_dtype):
    n = zb.shape[0]
    ch = zb.shape[1] // 2
    width = w.shape[0]
    nt = t // tt
    assert nt == 1 or tt >= CONV_HALO
    vec = pl.BlockSpec((1, ch), lambda i, j: (0, 0))
    return pl.pallas_call(
        _conv_body,
        out_shape=(jax.ShapeDtypeStruct((n, ch), out_dtype),
                   jax.ShapeDtypeStruct((b, width - 1, ch), F32)),
        grid=(b, nt),
        in_specs=[pl.BlockSpec((tt, ch), lambda i, j: (i * nt + j, 0)),
                  pl.BlockSpec((tt, ch), lambda i, j: (i * nt + j, 1)),
                  pl.BlockSpec((1, width - 1, ch), lambda i, j: (i, 0, 0)),
                  pl.BlockSpec((width, ch), lambda i, j: (0, 0)),
                  vec, vec, vec],
        out_specs=(pl.BlockSpec((tt, ch), lambda i, j: (i * nt + j, 0)),
                   pl.BlockSpec((1, width - 1, ch), lambda i, j: (i, 0, 0))),
        scratch_shapes=[pltpu.VMEM((CONV_HALO + tt, ch), F32), pltpu.VMEM((tt, ch), F32),
                        pltpu.VMEM((SUBLANES, CONV_HALO + tt, LANES), F32)],
        compiler_params=_cparams("arbitrary", "arbitrary"),
        name="conv_branch",
    )(zb, zb, buf, w, bias.reshape(1, ch), ln_g.reshape(1, ch), ln_b.reshape(1, ch))


def _fox_prompt_body(q_ref, k_ref, v_ref, c_ref, ct_ref, o_ref, *, tk, hb, dh):
    tq = q_ref.shape[0]
    scale = dh ** -0.5
    hp = pl.program_id(1)
    i = pl.program_id(2)
    lane = lax.broadcasted_iota(jnp.int32, (tq, LANES), 1)
    c_blk = c_ref[...]
    qs, cqs = [], []
    for hh in range(hb):
        qs.append((q_ref[:, hh * dh:(hh + 1) * dh] * scale).astype(BF16))
        cqs.append(jnp.sum(jnp.where(lane == hp * hb + hh, c_blk, 0.0), axis=1, keepdims=True))

    def kv_block(j, carry, masked):
        start = pl.multiple_of(j * tk, tk)
        out = []
        for hh in range(hb):
            m, l, acc = carry[hh]
            hs = slice(hh * dh, (hh + 1) * dh)
            ks = k_ref[pl.ds(start, tk), hs].astype(BF16)
            vs = v_ref[pl.ds(start, tk), hs].astype(BF16)
            ck = ct_ref[0, pl.ds(hp * hb + hh, 1), pl.ds(start, tk)]
            s = _dot_nt(qs[hh], ks) + cqs[hh] - ck
            if masked:
                qpos = i * tq + lax.broadcasted_iota(jnp.int32, (tq, tk), 0)
                kpos = j * tk + lax.broadcasted_iota(jnp.int32, (tq, tk), 1)
                s = jnp.where(kpos <= qpos, s, -jnp.inf)
            m_new = jnp.maximum(m, jnp.max(s, axis=1, keepdims=True))
            p = jnp.exp(s - m_new)
            alpha = jnp.exp(m - m_new)
            l = alpha * l + jnp.sum(p, axis=1, keepdims=True)
            acc = alpha * acc + _dot(p.astype(BF16), vs)
            out.append((m_new, l, acc))
        return tuple(out)

    n_full = (i * tq) // tk
    init = tuple((jnp.full((tq, 1), -jnp.inf, F32), jnp.zeros((tq, 1), F32), jnp.zeros((tq, dh), F32))
                 for _ in range(hb))
    carry = lax.fori_loop(0, n_full, lambda j, c: kv_block(j, c, False), init)
    carry = kv_block(n_full, carry, True)
    for hh in range(hb):
        _, l, acc = carry[hh]
        o_ref[:, hh * dh:(hh + 1) * dh] = (acc / l).astype(o_ref.dtype)


def fox_prompt(cq, ck, cv, c_pad, c_t, b, t, heads, tq, tk, hb):
    n = cq.shape[0]
    dh = cq.shape[1] // heads
    nq = t // tq
    assert tk % tq == 0 and t % tk == 0 and heads % hb == 0
    w = hb * dh
    return pl.pallas_call(
        functools.partial(_fox_prompt_body, tk=tk, hb=hb, dh=dh),
        out_shape=jax.ShapeDtypeStruct((n, heads * dh), BF16),
        grid=(b, heads // hb, nq),
        in_specs=[pl.BlockSpec((tq, w), lambda bi, h, i: (bi * nq + i, h)),
                  pl.BlockSpec((t, w), lambda bi, h, i: (bi, h)),
                  pl.BlockSpec((t, w), lambda bi, h, i: (bi, h)),
                  pl.BlockSpec((tq, LANES), lambda bi, h, i: (bi * nq + i, 0)),
                  pl.BlockSpec((1, c_t.shape[1], t), lambda bi, h, i: (bi, 0, 0))],
        out_specs=pl.BlockSpec((tq, w), lambda bi, h, i: (bi * nq + i, h)),
        compiler_params=_cparams("arbitrary", "arbitrary", "arbitrary"),
        name="fox_prompt",
    )(cq, ck, cv, c_pad, c_t)


def _fox_sample_body(pt_ref, q_ref, kn_ref, vn_ref, ccol_ref, crow_ref, *rest, heads, gp):
    kp_refs, vp_refs, lft_refs = rest[:gp], rest[gp:2 * gp], rest[2 * gp:3 * gp]
    o_ref, m_s, l_s, acc_s, carry_s = rest[3 * gp:]
    t, d = q_ref.shape
    dh = d // heads
    rows = lft_refs[0].shape[1]
    scale = dh ** -0.5
    p = pl.program_id(1)
    qb = [q_ref[:, h * dh:(h + 1) * dh].astype(BF16) for h in range(heads)]
    ccol = ccol_ref[0]

    def update(s, values_of):
        m_old = m_s[...]
        m_new = jnp.maximum(m_old, jnp.max(s, axis=1, keepdims=True))
        pe = jnp.exp(s - m_new)
        alpha = jnp.exp(m_old - m_new)
        l_s[...] = alpha * l_s[...] + jnp.sum(pe, axis=1, keepdims=True)
        pv = [_dot(pe[h * t:(h + 1) * t].astype(BF16), values_of(h)) for h in range(heads)]
        acc_s[...] = alpha * acc_s[...] + jnp.concatenate(pv, axis=0)
        m_s[...] = m_new

    @pl.when(p == 0)
    def _():
        carry_s[...] = jnp.zeros_like(carry_s)
        m_s[...] = jnp.full_like(m_s, -jnp.inf)
        l_s[...] = jnp.zeros_like(l_s)
        acc_s[...] = jnp.zeros_like(acc_s)
        pad = jnp.zeros((rows - t, dh), F32)
        sc = []
        for h in range(heads):
            kn = jnp.concatenate([kn_ref[:, h * dh:(h + 1) * dh], pad], axis=0)
            sc.append(_dot_nt(qb[h], kn.astype(BF16)) * scale)
        s = jnp.concatenate(sc, axis=0) + ccol - crow_ref[0]
        r = lax.broadcasted_iota(jnp.int32, (heads * t, rows), 0)
        c = lax.broadcasted_iota(jnp.int32, (heads * t, rows), 1)
        s = jnp.where(c <= r % t, s, -jnp.inf)
        update(s, lambda h: jnp.concatenate([vn_ref[:, h * dh:(h + 1) * dh], pad], axis=0).astype(BF16))

    rr = lax.broadcasted_iota(jnp.int32, (rows, rows), 0)
    cc = lax.broadcasted_iota(jnp.int32, (rows, rows), 1)
    later = jnp.where(rr > cc, 1.0, 0.0).astype(BF16)
    for g in range(gp):
        kp_ref, vp_ref = kp_refs[g], vp_refs[g]
        lft = lft_refs[g][...]
        d_t = _dot_exact_right(lft, later) + carry_s[...]
        carry_s[...] = carry_s[...] + jnp.sum(lft, axis=1, keepdims=True)
        sc = []
        for h in range(heads):
            kh = kp_ref[pl.ds(h, rows, stride=heads), :].astype(BF16)
            sc.append(_dot_nt(qb[h], kh) * scale + d_t[h:h + 1, :])
        s = jnp.concatenate(sc, axis=0) + ccol
        update(s, lambda h: vp_ref[pl.ds(h, rows, stride=heads), :].astype(BF16))

    @pl.when(p == pl.num_programs(1) - 1)
    def _():
        o = acc_s[...] / l_s[...]
        for h in range(heads):
            o_ref[:, h * dh:(h + 1) * dh] = o[h * t:(h + 1) * t]


def fox_sample(cq, ck_new, cv_new, c_col, c_row, cache_k, cache_v, cache_lft, layer, page_table, heads):
    n, d = cq.shape
    b, n_pages = page_table.shape
    t = n // b
    dh = d // heads
    rows = cache_lft.shape[3]
    gp = FOX_PAGES_PER_STEP
    assert rows == LANES and dh == LANES and n_pages % gp == 0
    tok = lambda bi, p, pt: (bi, 0)
    tab = lambda bi, p, pt: (bi, 0, 0)

    def page(g):
        return lambda bi, p, pt: (layer, pt[bi, n_pages - 1 - (gp * p + g)], 0, 0)

    kv_specs = [pl.BlockSpec((None, None, rows * heads, dh), page(g)) for g in range(gp)]
    lf_specs = [pl.BlockSpec((None, None, heads, rows), page(g)) for g in range(gp)]
    grid_spec = pltpu.PrefetchScalarGridSpec(
        num_scalar_prefetch=1,
        grid=(b, n_pages // gp),
        in_specs=[pl.BlockSpec((t, d), tok), pl.BlockSpec((t, d), tok), pl.BlockSpec((t, d), tok),
                  pl.BlockSpec((1, heads * t, LANES), tab), pl.BlockSpec((1, heads * t, LANES), tab)]
        + kv_specs + kv_specs + lf_specs,
        out_specs=pl.BlockSpec((t, d), tok),
        scratch_shapes=[pltpu.VMEM((heads * t, LANES), F32), pltpu.VMEM((heads * t, LANES), F32),
                        pltpu.VMEM((heads * t, dh), F32), pltpu.VMEM((heads, LANES), F32)],
    )
    return pl.pallas_call(
        functools.partial(_fox_sample_body, heads=heads, gp=gp),
        out_shape=jax.ShapeDtypeStruct((n, d), F32),
        grid_spec=grid_spec,
        compiler_params=_cparams("arbitrary", "arbitrary"),
        name="fox_sample",
    )(page_table, cq, ck_new, cv_new, c_col, c_row, *([cache_k] * gp), *([cache_v] * gp), *([cache_lft] * gp))


def _mem_attn_body(q_ref, k_ref, v_ref, o_ref, *, heads):
    d = q_ref.shape[1]
    dh = d // heads
    scale = dh ** -0.5
    for h in range(heads):
        hs = slice(h * dh, (h + 1) * dh)
        s = _dot_nt(q_ref[:, hs].astype(BF16), k_ref[0, :, hs].astype(BF16)) * scale
        m = jnp.max(s, axis=1, keepdims=True)
        pe = jnp.exp(s - m)
        o = _dot(pe.astype(BF16), v_ref[0, :, hs].astype(BF16)) / jnp.sum(pe, axis=1, keepdims=True)
        o_ref[:, hs] = o.astype(o_ref.dtype)


def mem_attention(qm, mem_k, mem_v, b, t, tt, heads, out_dtype):
    n, d = qm.shape
    nm = mem_k.shape[1]
    nt = t // tt
    return pl.pallas_call(
        functools.partial(_mem_attn_body, heads=heads),
        out_shape=jax.ShapeDtypeStruct((n, d), out_dtype),
        grid=(b, nt),
        in_specs=[pl.BlockSpec((tt, d), lambda i, j: (i * nt + j, 0)),
                  pl.BlockSpec((1, nm, d), lambda i, j: (i, 0, 0)),
                  pl.BlockSpec((1, nm, d), lambda i, j: (i, 0, 0))],
        out_specs=pl.BlockSpec((tt, d), lambda i, j: (i * nt + j, 0)),
        compiler_params=_cparams("arbitrary", "arbitrary"),
        name="mem_attention",
    )(qm, mem_k, mem_v)


def _ffn_act_body(g_ref, v_ref, buf_ref, w_ref, b_ref, o_ref, new_ref, g_s, sh_s):
    t, tc = g_ref.shape
    width = w_ref.shape[0]
    hist = width - 1
    g_s[FFN_HALO - hist:FFN_HALO, :] = buf_ref[0]
    g_s[FFN_HALO:FFN_HALO + t, :] = g_ref[...]
    acc = jnp.broadcast_to(b_ref[...], (t, tc))
    for j in range(width):
        start = FFN_HALO - hist + j
        if start % SUBLANES:
            sh_s[j] = g_s[pl.ds(start, t), :]
            xs = sh_s[j]
        else:
            xs = g_s[start:start + t, :]
        acc = acc + w_ref[j:j + 1, :] * xs
    o_ref[...] = (acc * _sigmoid(acc) * v_ref[...]).astype(o_ref.dtype)
    new_ref[0] = g_s[FFN_HALO + t - hist:FFN_HALO + t, :]


def ffn_act(up, buf, w, bias, b, t, tc, out_dtype):
    n = up.shape[0]
    dff = up.shape[1] // 2
    width = w.shape[0]
    nj = dff // tc
    return pl.pallas_call(
        _ffn_act_body,
        out_shape=(jax.ShapeDtypeStruct((n, dff), out_dtype),
                   jax.ShapeDtypeStruct((b, width - 1, dff), F32)),
        grid=(b, nj),
        in_specs=[pl.BlockSpec((t, tc), lambda i, j: (i, j)),
                  pl.BlockSpec((t, tc), lambda i, j: (i, nj + j)),
                  pl.BlockSpec((1, width - 1, tc), lambda i, j: (i, 0, j)),
                  pl.BlockSpec((width, tc), lambda i, j: (0, j)),
                  pl.BlockSpec((1, tc), lambda i, j: (0, j))],
        out_specs=(pl.BlockSpec((t, tc), lambda i, j: (i, j)),
                   pl.BlockSpec((1, width - 1, tc), lambda i, j: (i, 0, j))),
        scratch_shapes=[pltpu.VMEM((FFN_HALO + t, tc), F32), pltpu.VMEM((width, t, tc), F32)],
        compiler_params=_cparams("arbitrary", "arbitrary"),
        name="ffn_act",
    )(up, up, buf, w, bias.reshape(1, dff))


def _proj_tiles(n, k):
    if k <= 2048:
        return min(n, 1024), 1024
    return min(n, 512), 512


def _trunk_layer(x, b, t, l, w, p, lb, mem_k, mem_v, mem_heads, s0, conv_buf, ffn_buf, attend, prompt):
    n, d = x.shape
    heads_a = s0.shape[1]
    wa_cols = heads_a * LANES
    act = BF16 if prompt else F32
    tm, tn = _proj_tiles(n, d)
    w_in = w['w_in']
    off_b = 4 * wa_cols
    off_c = off_b + 2 * d
    off_f = off_c + 3 * d
    fox_heads = p['fox_f_bias'].shape[0]

    h1 = rmsnorm(x, p['norm_mix_g'], act, min(n, 512))
    za = matmul(h1, w_in, l, 0, off_b, tm=tm, tn=tn)
    zb = matmul(h1, w_in, l, off_b, 2 * d, tm=tm, tn=tn)
    cq = matmul(h1, w_in, l, off_c, d, tm=tm, tn=tn)
    ck = matmul(h1, w_in, l, off_c + d, d, tm=tm, tn=tn)
    cv = matmul(h1, w_in, l, off_c + 2 * d, d, tm=tm, tn=tn)
    gate = matmul(h1, w_in, l, off_f + fox_heads, 3 * d, tm=tm, tn=tn)
    bias_pad = jnp.zeros((1, LANES), F32).at[0, :fox_heads].set(p['fox_f_bias'])
    logf_pad, c_pad = fox_logf(h1, w_in, l, off_f // LANES, bias_pad, b, t, min(t, 256))

    oa, s_new = hgrn2(za, lb, p['hgrn_norm_g'], s0, b, t, act)
    ob, conv_new = conv_branch(zb, conv_buf, p['conv_dw_w'], p['conv_dw_b'], p['conv_ln_g'], p['conv_ln_b'],
                               b, t, min(t, 256), act)
    c_t = c_pad[:, :fox_heads].reshape(b, t, fox_heads).transpose(0, 2, 1)
    oc = attend(cq, ck, cv, c_pad, c_t)

    m = branch_merge(oa, ob, oc, w['w_branch_a'], w['w_branch_b'], w['w_branch_c'], l, gate,
                     tm=min(n, 256), tn=512)
    x = matmul(m, w['w_out'], l, 0, d, tm=tm, tn=tn, residual=x)

    h2 = rmsnorm(x, p['norm_mem_g'], act, min(n, 512))
    qm = matmul(h2, w['w_mq'], l, 0, d, tm=tm, tn=tn, out_dtype=act)
    om = mem_attention(qm, mem_k, mem_v, b, t, min(t, 512), mem_heads, act)
    x = matmul(om, w['w_mo'], l, 0, d, tm=tm, tn=tn, residual=x)

    h3 = rmsnorm(x, p['norm_ffn_g'], act, min(n, 512))
    dff = w['w_down'].shape[1]
    up = matmul(h3, w['w_up'], l, 0, 2 * dff, tm=tm, tn=tn)
    a, ffn_new = ffn_act(up, ffn_buf, p['ffn_dw_w'], p['ffn_dw_b'], b, t, 512, act)
    tm_d, tn_d = _proj_tiles(n, dff)
    x = matmul(a, w['w_down'], l, 0, d, tm=tm_d, tn=tn_d, residual=x)

    logf = logf_pad[:, :fox_heads].reshape(b, t, fox_heads)
    return x, (ck, cv, logf, s_new, conv_new, ffn_new)


def kernel(x_prompt, x_sample, cache_k, cache_v, cache_logf, page_table, cache_mem_k, cache_mem_v, state_hgrn, state_conv, state_ffn, mem_prompt, norm_mix_g, w_in, hgrn_lb_logits, hgrn_norm_g, conv_dw_w, conv_dw_b, conv_ln_g, conv_ln_b, fox_f_bias, w_branch_a, w_branch_b, w_branch_c, w_out, norm_mem_g, mem_kv_norm_g, w_mq, w_mk, w_mv, w_mo, norm_ffn_g, w_up, ffn_dw_w, ffn_dw_b, w_down, final_norm_g):
    depth = w_in.shape[0]
    bp, tp, d = x_prompt.shape
    bs, ts, _ = x_sample.shape
    n_mem = mem_prompt.shape[1]
    fox_heads = fox_f_bias.shape[1]
    dh = d // fox_heads
    hgrn_heads = state_hgrn.shape[2]
    n_pool, page_rows = cache_k.shape[1], cache_k.shape[2]
    mem_heads = cache_mem_k.shape[3]

    w = {'w_in': w_in, 'w_branch_a': w_branch_a, 'w_branch_b': w_branch_b,
         'w_branch_c': w_branch_c, 'w_out': w_out, 'w_mq': w_mq, 'w_mo': w_mo, 'w_up': w_up, 'w_down': w_down}

    ck_pages = cache_k.reshape(depth, n_pool, page_rows * fox_heads, dh)
    cv_pages = cache_v.reshape(depth, n_pool, page_rows * fox_heads, dh)
    lft_pages = cache_logf.astype(F32).transpose(0, 1, 3, 2)

    lbs = lower_bounds(hgrn_lb_logits.astype(F32))
    xp = x_prompt.reshape(bp * tp, d)
    xs = x_sample.reshape(bs * ts, d)
    memp = mem_prompt.reshape(bp * n_mem, d)
    rows_p, rows_s, mem_rows = [], [], []
    for l in range(depth):
        p = {'norm_mix_g': norm_mix_g[l], 'hgrn_norm_g': hgrn_norm_g[l],
             'conv_dw_w': conv_dw_w[l], 'conv_dw_b': conv_dw_b[l], 'conv_ln_g': conv_ln_g[l],
             'conv_ln_b': conv_ln_b[l], 'fox_f_bias': fox_f_bias[l],
             'norm_mem_g': norm_mem_g[l], 'norm_ffn_g': norm_ffn_g[l],
             'ffn_dw_w': ffn_dw_w[l], 'ffn_dw_b': ffn_dw_b[l]}
        lb = lbs[l]

        mn = rmsnorm(memp, mem_kv_norm_g[l], BF16, 512)
        tm_m, tn_m = _proj_tiles(bp * n_mem, d)
        mk = matmul(mn, w_mk, l, 0, d, tm=tm_m, tn=tn_m)
        mv = matmul(mn, w_mv, l, 0, d, tm=tm_m, tn=tn_m)
        mem_rows.append((mk, mv))
        s0 = jnp.zeros((bp, hgrn_heads, LANES, LANES), F32)
        cb0 = jnp.zeros((bp, conv_dw_w.shape[1] - 1, d), F32)
        fb0 = jnp.zeros((bp, ffn_dw_w.shape[1] - 1, w_down.shape[1]), F32)
        attend_p = lambda cq, ck, cv, c_pad, c_t: fox_prompt(cq, ck, cv, c_pad, c_t, bp, tp, fox_heads, 256, 512, 2)
        xp, new_p = _trunk_layer(xp, bp, tp, l, w, p, lb, mk.reshape(bp, n_mem, d), mv.reshape(bp, n_mem, d),
                                 mem_heads, s0, cb0, fb0, attend_p, True)
        rows_p.append(new_p)

        def attend_s(cq, ck, cv, c_pad, c_t, l=l):
            c_col = jnp.broadcast_to(c_t.reshape(bs, fox_heads * ts, 1), (bs, fox_heads * ts, LANES))
            c_row = jnp.repeat(jnp.pad(c_t, ((0, 0), (0, 0), (0, LANES - ts))), ts, axis=1)
            return fox_sample(cq, ck, cv, c_col, c_row, ck_pages, cv_pages, lft_pages, l, page_table, fox_heads)

        xs, new_s = _trunk_layer(xs, bs, ts, l, w, p, lb, cache_mem_k[l].reshape(bs, n_mem, d),
                                 cache_mem_v[l].reshape(bs, n_mem, d), mem_heads, state_hgrn[l],
                                 state_conv[l], state_ffn[l], attend_s, False)
        rows_s.append(new_s)

    y_prompt = rmsnorm(xp, final_norm_g, F32, 512).reshape(bp, tp, d)
    y_sample = rmsnorm(xs, final_norm_g, F32, bs * ts).reshape(bs, ts, d)

    def stack(rows, i, shape):
        return jnp.stack([r[i].reshape(shape) for r in rows])

    k_prompt = stack(rows_p, 0, (bp, tp, fox_heads, dh))
    v_prompt = stack(rows_p, 1, (bp, tp, fox_heads, dh))
    logf_prompt = stack(rows_p, 2, (bp, tp, fox_heads))
    hgrn_prompt = stack(rows_p, 3, (bp, hgrn_heads, LANES, LANES))
    conv_prompt = stack(rows_p, 4, (bp, conv_dw_w.shape[1] - 1, d))
    ffn_prompt = stack(rows_p, 5, (bp, ffn_dw_w.shape[1] - 1, w_down.shape[1]))
    mem_k_prompt = stack(mem_rows, 0, (bp, n_mem, mem_heads, d // mem_heads))
    mem_v_prompt = stack(mem_rows, 1, (bp, n_mem, mem_heads, d // mem_heads))
    k_sample = stack(rows_s, 0, (bs, ts, fox_heads, dh))
    v_sample = stack(rows_s, 1, (bs, ts, fox_heads, dh))
    logf_sample = stack(rows_s, 2, (bs, ts, fox_heads))
    hgrn_sample = stack(rows_s, 3, (bs, hgrn_heads, LANES, LANES))
    conv_sample = stack(rows_s, 4, (bs, conv_dw_w.shape[1] - 1, d))
    ffn_sample = stack(rows_s, 5, (bs, ffn_dw_w.shape[1] - 1, w_down.shape[1]))
    return (y_prompt, y_sample, k_prompt, v_prompt, logf_prompt, mem_k_prompt, mem_v_prompt, hgrn_prompt,
            conv_prompt, ffn_prompt, k_sample, v_sample, logf_sample, hgrn_sample, conv_sample, ffn_sample)
```

```python
import functools

import jax
import jax.numpy as jnp
from jax import lax
from jax.experimental import pallas as pl
from jax.experimental.pallas import tpu as pltpu

F32 = jnp.float32
BF16 = jnp.bfloat16

EPS = 1e-6
LANES = 128
SUBLANES = 8
VMEM_LIMIT = 56 * 1024 * 1024

HGRN_CHUNK = 128
HGRN_SUB = 16
HGRN_HEAD_BLOCK = 8
CONV_HALO = 32
FFN_HALO = 8
FOX_PAGES_PER_STEP = 4


def _cparams(*sem):
    return pltpu.CompilerParams(dimension_semantics=sem, vmem_limit_bytes=VMEM_LIMIT)


def _dot(a, b):
    return jnp.dot(a, b, preferred_element_type=F32)


def _dot_nt(a, b):
    return lax.dot_general(a, b, (((1,), (1,)), ((), ())), preferred_element_type=F32)


def _sigmoid(x):
    return 1.0 / (1.0 + jnp.exp(-x))


def _log_sigmoid(x):
    return jnp.minimum(x, 0.0) - jnp.log(1.0 + jnp.exp(-jnp.abs(x)))


def _split3(x):
    hi = x.astype(BF16)
    r = x - hi.astype(F32)
    mid = r.astype(BF16)
    lo = (r - mid.astype(F32)).astype(BF16)
    return hi, mid, lo


def _dot_exact_left(m, x):
    hi, mid, lo = _split3(x)
    return _dot(m, hi) + _dot(m, mid) + _dot(m, lo)


def _dot_exact_right(x, m):
    hi, mid, lo = _split3(x)
    return _dot(hi, m) + _dot(mid, m) + _dot(lo, m)


def _rms_body(x_ref, g_ref, o_ref):
    x = x_ref[...]
    ms = jnp.mean(x * x, axis=-1, keepdims=True)
    o_ref[...] = (x * lax.rsqrt(ms + EPS) * g_ref[...]).astype(o_ref.dtype)


def rmsnorm(x, g, out_dtype, tm):
    n, d = x.shape
    return pl.pallas_call(
        _rms_body,
        out_shape=jax.ShapeDtypeStruct((n, d), out_dtype),
        grid=(n // tm,),
        in_specs=[pl.BlockSpec((tm, d), lambda i: (i, 0)), pl.BlockSpec((1, d), lambda i: (0, 0))],
        out_specs=pl.BlockSpec((tm, d), lambda i: (i, 0)),
        compiler_params=_cparams("arbitrary"),
        name="rmsnorm",
    )(x, g.reshape(1, d))


def _mm_body(*refs, has_res, w_t, shift):
    refs = list(refs)
    x_ref, w_ref = refs[0], refs[1]
    if shift:
        w2_ref, tail_ref = refs[2:4]
        del refs[2:4]
    if has_res:
        r_ref, o_ref, wb_ref = refs[2:]
    else:
        o_ref, wb_ref = refs[2:]

    @pl.when(pl.program_id(1) == 0)
    def _():
        wt = w_ref[...]
        if shift:
            last = pl.program_id(0) == pl.num_programs(0) - 1
            nxt = jnp.where(last, tail_ref[...], w2_ref[...])
            wt = jnp.concatenate([wt, nxt], axis=0)[shift:shift + wt.shape[0]]
        if w_t:
            wt = wt.T
        wb_ref[...] = wt.astype(BF16)

    acc = _dot(x_ref[...].astype(BF16), wb_ref[...])
    if has_res:
        acc = acc + r_ref[...]
    o_ref[...] = acc.astype(o_ref.dtype)


def matmul(x, w, layer, col0, ncols, *, tm, tn, out_dtype=F32, residual=None, w_t=False):
    n, k = x.shape
    shift = col0 % LANES if w_t else 0
    col0 -= shift
    assert w.shape[2 if w_t else 1] == k and col0 % tn == 0 and ncols % tn == 0 and n % tm == 0
    assert shift % SUBLANES == 0
    c0 = col0 // tn
    if w_t:
        w_spec = pl.BlockSpec((None, tn, k), lambda j, i: (layer, c0 + j, 0))
    else:
        w_spec = pl.BlockSpec((None, k, tn), lambda j, i: (layer, 0, c0 + j))
    in_specs = [pl.BlockSpec((tm, k), lambda j, i: (i, 0)), w_spec]
    args = [x, w]
    if shift:
        per = tn // LANES
        nj = ncols // tn
        end = col0 + ncols
        assert end + shift <= w.shape[1]
        tail = jnp.pad(w[layer, end:end + shift, :], ((0, LANES - shift), (0, 0)))
        in_specs.append(pl.BlockSpec((None, LANES, k),
                                     lambda j, i: (layer, (c0 + jnp.minimum(j + 1, nj - 1)) * per, 0)))
        in_specs.append(pl.BlockSpec((LANES, k), lambda j, i: (0, 0)))
        args += [w, tail]
    if residual is not None:
        in_specs.append(pl.BlockSpec((tm, tn), lambda j, i: (i, j)))
        args.append(residual)
    return pl.pallas_call(
        functools.partial(_mm_body, has_res=residual is not None, w_t=w_t, shift=shift),
        out_shape=jax.ShapeDtypeStruct((n, ncols), out_dtype),
        grid=(ncols // tn, n // tm),
        in_specs=in_specs,
        out_specs=pl.BlockSpec((tm, tn), lambda j, i: (i, j)),
        scratch_shapes=[pltpu.VMEM((k, tn), BF16)],
        compiler_params=_cparams("arbitrary", "arbitrary"),
        name="matmul",
    )(*args)


def _merge_body(oa_ref, ob_ref, oc_ref, wa_ref, wb_ref, wc_ref, ga_ref, gb_ref, gc_ref, o_ref,
                wa_s, wb_s, wc_s):
    @pl.when(pl.program_id(1) == 0)
    def _():
        wa_s[...] = wa_ref[...].astype(BF16)
        wb_s[...] = wb_ref[...].astype(BF16)
        wc_s[...] = wc_ref[...].astype(BF16)

    m = _sigmoid(ga_ref[...]) * _dot(oa_ref[...].astype(BF16), wa_s[...])
    m = m + _sigmoid(gb_ref[...]) * _dot(ob_ref[...].astype(BF16), wb_s[...])
    m = m + _sigmoid(gc_ref[...]) * _dot(oc_ref[...].astype(BF16), wc_s[...])
    o_ref[...] = m.astype(o_ref.dtype)


def branch_merge(oa, ob, oc, wa, wb, wc, layer, gate, *, tm, tn):
    n, d = oa.shape
    nb = d // tn
    act = pl.BlockSpec((tm, d), lambda j, i: (i, 0))
    wsp = pl.BlockSpec((None, d, tn), lambda j, i: (layer, 0, j))
    gsp = [pl.BlockSpec((tm, tn), lambda j, i, b=b: (i, b * nb + j)) for b in range(3)]
    return pl.pallas_call(
        _merge_body,
        out_shape=jax.ShapeDtypeStruct((n, d), BF16),
        grid=(nb, n // tm),
        in_specs=[act, act, act, wsp, wsp, wsp] + gsp,
        out_specs=pl.BlockSpec((tm, tn), lambda j, i: (i, j)),
        scratch_shapes=[pltpu.VMEM((d, tn), BF16)] * 3,
        compiler_params=_cparams("arbitrary", "arbitrary"),
        name="branch_merge",
    )(oa, ob, oc, wa, wb, wc, gate, gate, gate)


def _lb_body(x_ref, o_ref):
    x = x_ref[...]
    depth = x.shape[0]
    mx = jnp.max(x, axis=0, keepdims=True)
    e = jnp.exp(x - mx)
    soft = e / jnp.sum(e, axis=0, keepdims=True)
    run = jnp.zeros_like(soft[0:1])
    o_ref[0:1, :] = run
    for l in range(1, depth):
        run = run + soft[l:l + 1]
        o_ref[l:l + 1, :] = run


def lower_bounds(logits):
    return pl.pallas_call(
        _lb_body, out_shape=jax.ShapeDtypeStruct(logits.shape, F32), name="hgrn_lower_bounds",
    )(logits)


def _logf_body(h_ref, w_ref, b_ref, lf_ref, c_ref, carry):
    t = h_ref.shape[0]
    rows = max(t, LANES)

    @pl.when(pl.program_id(1) == 0)
    def _():
        carry[...] = jnp.zeros_like(carry)

    z = _dot_nt(h_ref[...].astype(BF16), w_ref[...].astype(BF16)) + b_ref[...]
    lf = _log_sigmoid(z)
    lf_ref[...] = lf
    if rows > t:
        lf = jnp.concatenate([lf, jnp.zeros((rows - t, LANES), F32)], axis=0)
    r = lax.broadcasted_iota(jnp.int32, (rows, rows), 0)
    s = lax.broadcasted_iota(jnp.int32, (rows, rows), 1)
    tril = jnp.where(s <= r, 1.0, 0.0).astype(BF16)
    c = _dot_exact_left(tril, lf) + carry[...]
    c_ref[...] = c[:t]
    carry[...] = c[rows - 1:rows]


def fox_logf(h, w_in, layer, col_block, bias_pad, b, t, tt):
    n, k = h.shape
    nt = t // tt
    return pl.pallas_call(
        _logf_body,
        out_shape=(jax.ShapeDtypeStruct((n, LANES), F32), jax.ShapeDtypeStruct((n, LANES), F32)),
        grid=(b, nt),
        in_specs=[pl.BlockSpec((tt, k), lambda i, j: (i * nt + j, 0)),
                  pl.BlockSpec((None, LANES, k), lambda i, j: (layer, col_block, 0)),
                  pl.BlockSpec((1, LANES), lambda i, j: (0, 0))],
        out_specs=(pl.BlockSpec((tt, LANES), lambda i, j: (i * nt + j, 0)),
                   pl.BlockSpec((tt, LANES), lambda i, j: (i * nt + j, 0))),
        scratch_shapes=[pltpu.VMEM((1, LANES), F32)],
        compiler_params=_cparams("arbitrary", "arbitrary"),
        name="fox_logf",
    )(h, w_in, bias_pad)


def _hgrn_body(aq_ref, af_ref, ai_ref, ag_ref, lb_ref, ng_ref, s0_ref, o_ref, s_ref,
               st, g_s, q_s, k_s, v_s, o_s, *, hb):
    L, C = HGRN_CHUNK, HGRN_SUB
    tb = aq_ref.shape[0]
    c = pl.program_id(2)

    @pl.when(c == 0)
    def _():
        for h in range(hb):
            st[h] = s0_ref[0, h].T

    r = lax.broadcasted_iota(jnp.int32, (L, L), 0)
    s = lax.broadcasted_iota(jnp.int32, (L, L), 1)
    tril = jnp.where(s <= r, 1.0, 0.0).astype(BF16)
    row = lax.broadcasted_iota(jnp.int32, (L, 1), 0)

    for h in range(hb):
        ls = slice(h * LANES, (h + 1) * LANES)
        lb = lb_ref[:, ls]
        qr = aq_ref[:, ls]
        q = qr * _sigmoid(qr)
        fg = lb + (1.0 - lb) * _sigmoid(af_ref[:, ls])
        kk = 1.0 - fg
        gl = jnp.log(fg)
        v = ai_ref[:, ls]
        if tb < L:
            pad = jnp.zeros((L - tb, LANES), F32)
            q, kk, gl, v = (jnp.concatenate([a, pad], axis=0) for a in (q, kk, gl, v))

        G = _dot_exact_left(tril, gl)
        st_old = st[h]

        o = _dot_nt((q * jnp.exp(G)).astype(BF16), st_old.astype(BF16))

        a_off = jnp.zeros((L, L), F32)
        blk = C
        while blk < L:
            grp = 2 * blk
            gref = jnp.concatenate(
                [jnp.broadcast_to(G[g0 + blk - 1:g0 + blk, :], (grp, LANES)) for g0 in range(0, L, grp)],
                axis=0)
            upper = (row // blk) % 2 == 1
            qt = q * jnp.exp(jnp.where(upper, G - gref, -jnp.inf))
            kt = kk * jnp.exp(jnp.where(upper, -jnp.inf, gref - G))
            a = _dot_nt(qt.astype(BF16), kt.astype(BF16))
            a_off = a_off + jnp.where(r // grp == s // grp, a, 0.0)
            blk = grp
        o_s[h] = o + _dot(a_off.astype(BF16), v.astype(BF16))

        g_s[h] = G
        q_s[h] = q
        k_s[h] = kk
        v_s[h] = v

        gl_last = G[L - 1:L, :]
        k_dec = kk * jnp.exp(gl_last - G)
        st[h] = st_old * jnp.exp(gl_last) + _dot(v.T.astype(BF16), k_dec.astype(BF16))

    ones = jnp.ones((LANES, LANES), BF16)
    rowc = lax.broadcasted_iota(jnp.int32, (C, LANES), 0)

    def sub_block(j, carry):
        base = pl.multiple_of(j * C, C)
        for h in range(hb):
            gj = g_s[h, pl.ds(base, C), :]
            qj = q_s[h, pl.ds(base, C), :]
            parts = []
            for si in range(C):
                gs = g_s[h, pl.ds(base + si, 1), :]
                ks = k_s[h, pl.ds(base + si, 1), :]
                e = jnp.exp(jnp.where(rowc >= si, gj - gs, -jnp.inf))
                parts.append(qj * e * ks)
            rs = _dot(jnp.concatenate(parts, axis=0).astype(BF16), ones)
            acc = o_s[h, pl.ds(base, C), :]
            for si in range(C):
                acc = acc + rs[si * C:(si + 1) * C] * v_s[h, pl.ds(base + si, 1), :]
            o_s[h, pl.ds(base, C), :] = acc
        return carry

    lax.fori_loop(0, L // C, sub_block, 0)

    for h in range(hb):
        ls = slice(h * LANES, (h + 1) * LANES)
        o = o_s[h][:tb]
        ms = jnp.mean(o * o, axis=-1, keepdims=True)
        o = o * lax.rsqrt(ms + EPS) * ng_ref[...] * _sigmoid(ag_ref[:, ls])
        o_ref[:, ls] = o.astype(o_ref.dtype)

    @pl.when(c == pl.num_programs(2) - 1)
    def _():
        for h in range(hb):
            s_ref[0, h] = st[h].T


def hgrn2(za, lb, norm_g, s0, b, t, out_dtype):
    n = za.shape[0]
    heads = s0.shape[1]
    assert s0.shape[2:] == (LANES, LANES)
    hb = min(HGRN_HEAD_BLOCK, heads)
    nhb = heads // hb
    tb = min(t, HGRN_CHUNK)
    nc = t // tb
    w = hb * LANES
    col = lambda part: pl.BlockSpec((tb, w), lambda i, h, c, part=part: (i * nc + c, part * nhb + h))
    return pl.pallas_call(
        functools.partial(_hgrn_body, hb=hb),
        out_shape=(jax.ShapeDtypeStruct((n, heads * LANES), out_dtype),
                   jax.ShapeDtypeStruct(s0.shape, F32)),
        grid=(b, nhb, nc),
        in_specs=[col(0), col(1), col(2), col(3),
                  pl.BlockSpec((1, w), lambda i, h, c: (0, h)),
                  pl.BlockSpec((1, LANES), lambda i, h, c: (0, 0)),
                  pl.BlockSpec((1, hb, LANES, LANES), lambda i, h, c: (i, h, 0, 0))],
        out_specs=(pl.BlockSpec((tb, w), lambda i, h, c: (i * nc + c, h)),
                   pl.BlockSpec((1, hb, LANES, LANES), lambda i, h, c: (i, h, 0, 0))),
        scratch_shapes=[pltpu.VMEM((hb, LANES, LANES), F32)] + [pltpu.VMEM((hb, HGRN_CHUNK, LANES), F32)] * 5,
        compiler_params=_cparams("arbitrary", "arbitrary", "arbitrary"),
        name="hgrn2",
    )(za, za, za, za, lb.reshape(1, -1), norm_g.reshape(1, -1), s0)


def _conv_body(x1_ref, x2_ref, buf_ref, w_ref, b_ref, lg_ref, lbias_ref, o_ref, new_ref, u_s, y_s, sh_s):
    tt, ch = x1_ref.shape
    width = w_ref.shape[0]
    hist = width - 1
    i = pl.program_id(1)

    @pl.when(i == 0)
    def _():
        u_s[0:CONV_HALO - hist, :] = jnp.zeros((CONV_HALO - hist, ch), F32)
        u_s[CONV_HALO - hist:CONV_HALO, :] = buf_ref[0]

    u_s[CONV_HALO:CONV_HALO + tt, :] = x1_ref[...] * _sigmoid(x2_ref[...])

    def lane_block(cb, carry):
        ls = pl.ds(pl.multiple_of(cb * LANES, LANES), LANES)
        acc = jnp.broadcast_to(b_ref[:, ls], (tt, LANES))
        for s in range(SUBLANES):
            taps = [j for j in range(width) if (CONV_HALO - hist + j) % SUBLANES == s]
            if not taps:
                continue
            span = max(CONV_HALO - hist + j - s for j in taps) + tt
            if s:
                sh_s[s, 0:span, :] = u_s[pl.ds(s, span), ls]
            for j in taps:
                off = CONV_HALO - hist + j - s
                xs = sh_s[s, off:off + tt, :] if s else u_s[off:off + tt, ls]
                acc = acc + w_ref[j:j + 1, ls] * xs
        y_s[:, ls] = acc
        return carry

    lax.fori_loop(0, ch // LANES, lane_block, 0)

    y = y_s[...]
    mu = jnp.mean(y, axis=-1, keepdims=True)
    yc = y - mu
    var = jnp.mean(yc * yc, axis=-1, keepdims=True)
    yn = yc * lax.rsqrt(var + EPS) * lg_ref[...] + lbias_ref[...]
    o_ref[...] = (yn * _sigmoid(yn)).astype(o_ref.dtype)

    @pl.when(i == pl.num_programs(1) - 1)
    def _():
        new_ref[0] = u_s[CONV_HALO + tt - hist:CONV_HALO + tt, :]

    if tt >= CONV_HALO:
        @pl.when(i < pl.num_programs(1) - 1)
        def _():
            u_s[0:CONV_HALO, :] = u_s[tt:tt + CONV_HALO, :]


def conv_branch(zb, buf, w, bias, ln_g, ln_b, b, t, tt, out_dtype):
    n = zb.shape[0]
    ch = zb.shape[1] // 2
    width = w.shape[0]
    nt = t // tt
    assert nt == 1 or tt >= CONV_HALO
    vec = pl.BlockSpec((1, ch), lambda i, j: (0, 0))
    return pl.pallas_call(
        _conv_body,
        out_shape=(jax.ShapeDtypeStruct((n, ch), out_dtype),
                   jax.ShapeDtypeStruct((b, width - 1, ch), F32)),
        grid=(b, nt),
        in_specs=[pl.BlockSpec((tt, ch), lambda i, j: (i * nt + j, 0)),
                  pl.BlockSpec((tt, ch), lambda i, j: (i * nt + j, 1)),
                  pl.BlockSpec((1, width - 1, ch), lambda i, j: (i, 0, 0)),
                  pl.BlockSpec((width, ch), lambda i, j: (0, 0)),
                  vec, vec, vec],
        out_specs=(pl.BlockSpec((tt, ch), lambda i, j: (i * nt + j, 0)),
                   pl.BlockSpec((1, width - 1, ch), lambda i, j: (i, 0, 0))),
        scratch_shapes=[pltpu.VMEM((CONV_HALO + tt, ch), F32), pltpu.VMEM((tt, ch), F32),
                        pltpu.VMEM((SUBLANES, CONV_HALO + tt, LANES), F32)],
        compiler_params=_cparams("arbitrary", "arbitrary"),
        name="conv_branch",
    )(zb, zb, buf, w, bias.reshape(1, ch), ln_g.reshape(1, ch), ln_b.reshape(1, ch))


def _fox_prompt_body(q_ref, k_ref, v_ref, c_ref, ct_ref, o_ref, *, tk, hb, dh):
    tq = q_ref.shape[0]
    scale = dh ** -0.5
    hp = pl.program_id(1)
    i = pl.program_id(2)
    lane = lax.broadcasted_iota(jnp.int32, (tq, LANES), 1)
    c_blk = c_ref[...]
    qs, cqs = [], []
    for hh in range(hb):
        qs.append((q_ref[:, hh * dh:(hh + 1) * dh] * scale).astype(BF16))
        cqs.append(jnp.sum(jnp.where(lane == hp * hb + hh, c_blk, 0.0), axis=1, keepdims=True))

    def kv_block(j, carry, masked):
        start = pl.multiple_of(j * tk, tk)
        out = []
        for hh in range(hb):
            m, l, acc = carry[hh]
            hs = slice(hh * dh, (hh + 1) * dh)
            ks = k_ref[pl.ds(start, tk), hs].astype(BF16)
            vs = v_ref[pl.ds(start, tk), hs].astype(BF16)
            ck = ct_ref[0, pl.ds(hp * hb + hh, 1), pl.ds(start, tk)]
            s = _dot_nt(qs[hh], ks) + cqs[hh] - ck
            if masked:
                qpos = i * tq + lax.broadcasted_iota(jnp.int32, (tq, tk), 0)
                kpos = j * tk + lax.broadcasted_iota(jnp.int32, (tq, tk), 1)
                s = jnp.where(kpos <= qpos, s, -jnp.inf)
            m_new = jnp.maximum(m, jnp.max(s, axis=1, keepdims=True))
            p = jnp.exp(s - m_new)
            alpha = jnp.exp(m - m_new)
            l = alpha * l + jnp.sum(p, axis=1, keepdims=True)
            acc = alpha * acc + _dot(p.astype(BF16), vs)
            out.append((m_new, l, acc))
        return tuple(out)

    n_full = (i * tq) // tk
    init = tuple((jnp.full((tq, 1), -jnp.inf, F32), jnp.zeros((tq, 1), F32), jnp.zeros((tq, dh), F32))
                 for _ in range(hb))
    carry = lax.fori_loop(0, n_full, lambda j, c: kv_block(j, c, False), init)
    carry = kv_block(n_full, carry, True)
    for hh in range(hb):
        _, l, acc = carry[hh]
        o_ref[:, hh * dh:(hh + 1) * dh] = (acc / l).astype(o_ref.dtype)


def fox_prompt(cq, ck, cv, c_pad, c_t, b, t, heads, tq, tk, hb):
    n = cq.shape[0]
    dh = cq.shape[1] // heads
    nq = t // tq
    assert tk % tq == 0 and t % tk == 0 and heads % hb == 0
    w = hb * dh
    return pl.pallas_call(
        functools.partial(_fox_prompt_body, tk=tk, hb=hb, dh=dh),
        out_shape=jax.ShapeDtypeStruct((n, heads * dh), BF16),
        grid=(b, heads // hb, nq),
        in_specs=[pl.BlockSpec((tq, w), lambda bi, h, i: (bi * nq + i, h)),
                  pl.BlockSpec((t, w), lambda bi, h, i: (bi, h)),
                  pl.BlockSpec((t, w), lambda bi, h, i: (bi, h)),
                  pl.BlockSpec((tq, LANES), lambda bi, h, i: (bi * nq + i, 0)),
                  pl.BlockSpec((1, c_t.shape[1], t), lambda bi, h, i: (bi, 0, 0))],
        out_specs=pl.BlockSpec((tq, w), lambda bi, h, i: (bi * nq + i, h)),
        compiler_params=_cparams("arbitrary", "arbitrary", "arbitrary"),
        name="fox_prompt",
    )(cq, ck, cv, c_pad, c_t)


def _fox_sample_body(pt_ref, q_ref, kn_ref, vn_ref, ccol_ref, crow_ref, *rest, heads, gp):
    kp_refs, vp_refs, lft_refs = rest[:gp], rest[gp:2 * gp], rest[2 * gp:3 * gp]
    o_ref, m_s, l_s, acc_s, carry_s = rest[3 * gp:]
    t, d = q_ref.shape
    dh = d // heads
    rows = lft_refs[0].shape[1]
    scale = dh ** -0.5
    p = pl.program_id(1)
    qb = [q_ref[:, h * dh:(h + 1) * dh].astype(BF16) for h in range(heads)]
    ccol = ccol_ref[0]

    def update(s, values_of):
        m_old = m_s[...]
        m_new = jnp.maximum(m_old, jnp.max(s, axis=1, keepdims=True))
        pe = jnp.exp(s - m_new)
        alpha = jnp.exp(m_old - m_new)
        l_s[...] = alpha * l_s[...] + jnp.sum(pe, axis=1, keepdims=True)
        pv = [_dot(pe[h * t:(h + 1) * t].astype(BF16), values_of(h)) for h in range(heads)]
        acc_s[...] = alpha * acc_s[...] + jnp.concatenate(pv, axis=0)
        m_s[...] = m_new

    @pl.when(p == 0)
    def _():
        carry_s[...] = jnp.zeros_like(carry_s)
        m_s[...] = jnp.full_like(m_s, -jnp.inf)
        l_s[...] = jnp.zeros_like(l_s)
        acc_s[...] = jnp.zeros_like(acc_s)
        pad = jnp.zeros((rows - t, dh), F32)
        sc = []
        for h in range(heads):
            kn = jnp.concatenate([kn_ref[:, h * dh:(h + 1) * dh], pad], axis=0)
            sc.append(_dot_nt(qb[h], kn.astype(BF16)) * scale)
        s = jnp.concatenate(sc, axis=0) + ccol - crow_ref[0]
        r = lax.broadcasted_iota(jnp.int32, (heads * t, rows), 0)
        c = lax.broadcasted_iota(jnp.int32, (heads * t, rows), 1)
        s = jnp.where(c <= r % t, s, -jnp.inf)
        update(s, lambda h: jnp.concatenate([vn_ref[:, h * dh:(h + 1) * dh], pad], axis=0).astype(BF16))

    rr = lax.broadcasted_iota(jnp.int32, (rows, rows), 0)
    cc = lax.broadcasted_iota(jnp.int32, (rows, rows), 1)
    later = jnp.where(rr > cc, 1.0, 0.0).astype(BF16)
    for g in range(gp):
        kp_ref, vp_ref = kp_refs[g], vp_refs[g]
        lft = lft_refs[g][...]
        d_t = _dot_exact_right(lft, later) + carry_s[...]
        carry_s[...] = carry_s[...] + jnp.sum(lft, axis=1, keepdims=True)
        sc = []
        for h in range(heads):
            kh = kp_ref[pl.ds(h, rows, stride=heads), :].astype(BF16)
            sc.append(_dot_nt(qb[h], kh) * scale + d_t[h:h + 1, :])
        s = jnp.concatenate(sc, axis=0) + ccol
        update(s, lambda h: vp_ref[pl.ds(h, rows, stride=heads), :].astype(BF16))

    @pl.when(p == pl.num_programs(1) - 1)
    def _():
        o = acc_s[...] / l_s[...]
        for h in range(heads):
            o_ref[:, h * dh:(h + 1) * dh] = o[h * t:(h + 1) * t]


def fox_sample(cq, ck_new, cv_new, c_col, c_row, cache_k, cache_v, cache_lft, layer, page_table, heads):
    n, d = cq.shape
    b, n_pages = page_table.shape
    t = n // b
    dh = d // heads
    rows = cache_lft.shape[3]
    gp = FOX_PAGES_PER_STEP
    assert rows == LANES and dh == LANES and n_pages % gp == 0
    tok = lambda bi, p, pt: (bi, 0)
    tab = lambda bi, p, pt: (bi, 0, 0)

    def page(g):
        return lambda bi, p, pt: (layer, pt[bi, n_pages - 1 - (gp * p + g)], 0, 0)

    kv_specs = [pl.BlockSpec((None, None, rows * heads, dh), page(g)) for g in range(gp)]
    lf_specs = [pl.BlockSpec((None, None, heads, rows), page(g)) for g in range(gp)]
    grid_spec = pltpu.PrefetchScalarGridSpec(
        num_scalar_prefetch=1,
        grid=(b, n_pages // gp),
        in_specs=[pl.BlockSpec((t, d), tok), pl.BlockSpec((t, d), tok), pl.BlockSpec((t, d), tok),
                  pl.BlockSpec((1, heads * t, LANES), tab), pl.BlockSpec((1, heads * t, LANES), tab)]
        + kv_specs + kv_specs + lf_specs,
        out_specs=pl.BlockSpec((t, d), tok),
        scratch_shapes=[pltpu.VMEM((heads * t, LANES), F32), pltpu.VMEM((heads * t, LANES), F32),
                        pltpu.VMEM((heads * t, dh), F32), pltpu.VMEM((heads, LANES), F32)],
    )
    return pl.pallas_call(
        functools.partial(_fox_sample_body, heads=heads, gp=gp),
        out_shape=jax.ShapeDtypeStruct((n, d), F32),
        grid_spec=grid_spec,
        compiler_params=_cparams("arbitrary", "arbitrary"),
        name="fox_sample",
    )(page_table, cq, ck_new, cv_new, c_col, c_row, *([cache_k] * gp), *([cache_v] * gp), *([cache_lft] * gp))


def _mem_attn_body(q_ref, k_ref, v_ref, o_ref, *, heads):
    d = q_ref.shape[1]
    dh = d // heads
    scale = dh ** -0.5
    for h in range(heads):
        hs = slice(h * dh, (h + 1) * dh)
        s = _dot_nt(q_ref[:, hs].astype(BF16), k_ref[0, :, hs].astype(BF16)) * scale
        m = jnp.max(s, axis=1, keepdims=True)
        pe = jnp.exp(s - m)
        o = _dot(pe.astype(BF16), v_ref[0, :, hs].astype(BF16)) / jnp.sum(pe, axis=1, keepdims=True)
        o_ref[:, hs] = o.astype(o_ref.dtype)


def mem_attention(qm, mem_k, mem_v, b, t, tt, heads, out_dtype):
    n, d = qm.shape
    nm = mem_k.shape[1]
    nt = t // tt
    return pl.pallas_call(
        functools.partial(_mem_attn_body, heads=heads),
        out_shape=jax.ShapeDtypeStruct((n, d), out_dtype),
        grid=(b, nt),
        in_specs=[pl.BlockSpec((tt, d), lambda i, j: (i * nt + j, 0)),
                  pl.BlockSpec((1, nm, d), lambda i, j: (i, 0, 0)),
                  pl.BlockSpec((1, nm, d), lambda i, j: (i, 0, 0))],
        out_specs=pl.BlockSpec((tt, d), lambda i, j: (i * nt + j, 0)),
        compiler_params=_cparams("arbitrary", "arbitrary"),
        name="mem_attention",
    )(qm, mem_k, mem_v)


def _ffn_act_body(g_ref, v_ref, buf_ref, w_ref, b_ref, o_ref, new_ref, g_s, sh_s):
    t, tc = g_ref.shape
    width = w_ref.shape[0]
    hist = width - 1
    g_s[FFN_HALO - hist:FFN_HALO, :] = buf_ref[0]
    g_s[FFN_HALO:FFN_HALO + t, :] = g_ref[...]
    acc = jnp.broadcast_to(b_ref[...], (t, tc))
    for j in range(width):
        start = FFN_HALO - hist + j
        if start % SUBLANES:
            sh_s[j] = g_s[pl.ds(start, t), :]
            xs = sh_s[j]
        else:
            xs = g_s[start:start + t, :]
        acc = acc + w_ref[j:j + 1, :] * xs
    o_ref[...] = (acc * _sigmoid(acc) * v_ref[...]).astype(o_ref.dtype)
    new_ref[0] = g_s[FFN_HALO + t - hist:FFN_HALO + t, :]


def ffn_act(up, buf, w, bias, b, t, tc, out_dtype):
    n = up.shape[0]
    dff = up.shape[1] // 2
    width = w.shape[0]
    nj = dff // tc
    return pl.pallas_call(
        _ffn_act_body,
        out_shape=(jax.ShapeDtypeStruct((n, dff), out_dtype),
                   jax.ShapeDtypeStruct((b, width - 1, dff), F32)),
        grid=(b, nj),
        in_specs=[pl.BlockSpec((t, tc), lambda i, j: (i, j)),
                  pl.BlockSpec((t, tc), lambda i, j: (i, nj + j)),
                  pl.BlockSpec((1, width - 1, tc), lambda i, j: (i, 0, j)),
                  pl.BlockSpec((width, tc), lambda i, j: (0, j)),
                  pl.BlockSpec((1, tc), lambda i, j: (0, j))],
        out_specs=(pl.BlockSpec((t, tc), lambda i, j: (i, j)),
                   pl.BlockSpec((1, width - 1, tc), lambda i, j: (i, 0, j))),
        scratch_shapes=[pltpu.VMEM((FFN_HALO + t, tc), F32), pltpu.VMEM((width, t, tc), F32)],
        compiler_params=_cparams("arbitrary", "arbitrary"),
        name="ffn_act",
    )(up, up, buf, w, bias.reshape(1, dff))


def _proj_tiles(n, k):
    if k <= 2048:
        return min(n, 1024), 1024
    return min(n, 512), 512


def _trunk_layer(x, b, t, l, w, p, lb, mem_k, mem_v, mem_heads, s0, conv_buf, ffn_buf, attend, prompt):
    n, d = x.shape
    heads_a = s0.shape[1]
    wa_cols = heads_a * LANES
    act = BF16 if prompt else F32
    tm, tn = _proj_tiles(n, d)
    w_in = w['w_in']
    off_b = 4 * wa_cols
    off_c = off_b + 2 * d
    off_f = off_c + 3 * d
    fox_heads = p['fox_f_bias'].shape[0]

    h1 = rmsnorm(x, p['norm_mix_g'], act, min(n, 512))
    za = matmul(h1, w_in, l, 0, off_b, tm=tm, tn=tn, w_t=True)
    zb = matmul(h1, w_in, l, off_b, 2 * d, tm=tm, tn=tn, w_t=True)
    cq = matmul(h1, w_in, l, off_c, d, tm=tm, tn=tn, w_t=True)
    ck = matmul(h1, w_in, l, off_c + d, d, tm=tm, tn=tn, w_t=True)
    cv = matmul(h1, w_in, l, off_c + 2 * d, d, tm=tm, tn=tn, w_t=True)
    gate = matmul(h1, w_in, l, off_f + fox_heads, 3 * d, tm=tm, tn=tn, w_t=True)
    bias_pad = jnp.zeros((1, LANES), F32).at[0, :fox_heads].set(p['fox_f_bias'])
    logf_pad, c_pad = fox_logf(h1, w_in, l, off_f // LANES, bias_pad, b, t, min(t, 256))

    oa, s_new = hgrn2(za, lb, p['hgrn_norm_g'], s0, b, t, act)
    ob, conv_new = conv_branch(zb, conv_buf, p['conv_dw_w'], p['conv_dw_b'], p['conv_ln_g'], p['conv_ln_b'],
                               b, t, min(t, 256), act)
    c_t = c_pad[:, :fox_heads].reshape(b, t, fox_heads).transpose(0, 2, 1)
    oc = attend(cq, ck, cv, c_pad, c_t)

    m = branch_merge(oa, ob, oc, w['w_branch_a'], w['w_branch_b'], w['w_branch_c'], l, gate,
                     tm=min(n, 256), tn=512)
    x = matmul(m, w['w_out'], l, 0, d, tm=tm, tn=tn, residual=x)

    h2 = rmsnorm(x, p['norm_mem_g'], act, min(n, 512))
    qm = matmul(h2, w['w_mq'], l, 0, d, tm=tm, tn=tn, out_dtype=act)
    om = mem_attention(qm, mem_k, mem_v, b, t, min(t, 512), mem_heads, act)
    x = matmul(om, w['w_mo'], l, 0, d, tm=tm, tn=tn, residual=x)

    h3 = rmsnorm(x, p['norm_ffn_g'], act, min(n, 512))
    dff = w['w_down'].shape[1]
    up = matmul(h3, w['w_up'], l, 0, 2 * dff, tm=tm, tn=tn)
    a, ffn_new = ffn_act(up, ffn_buf, p['ffn_dw_w'], p['ffn_dw_b'], b, t, 512, act)
    tm_d, tn_d = _proj_tiles(n, dff)
    x = matmul(a, w['w_down'], l, 0, d, tm=tm_d, tn=tn_d, residual=x)

    logf = logf_pad[:, :fox_heads].reshape(b, t, fox_heads)
    return x, (ck, cv, logf, s_new, conv_new, ffn_new)


def kernel(x_prompt, x_sample, cache_k, cache_v, cache_logf, page_table, cache_mem_k, cache_mem_v, state_hgrn, state_conv, state_ffn, mem_prompt, norm_mix_g, w_in, hgrn_lb_logits, hgrn_norm_g, conv_dw_w, conv_dw_b, conv_ln_g, conv_ln_b, fox_f_bias, w_branch_a, w_branch_b, w_branch_c, w_out, norm_mem_g, mem_kv_norm_g, w_mq, w_mk, w_mv, w_mo, norm_ffn_g, w_up, ffn_dw_w, ffn_dw_b, w_down, final_norm_g):
    depth = w_in.shape[0]
    bp, tp, d = x_prompt.shape
    bs, ts, _ = x_sample.shape
    n_mem = mem_prompt.shape[1]
    fox_heads = fox_f_bias.shape[1]
    dh = d // fox_heads
    hgrn_heads = state_hgrn.shape[2]
    n_pool, page_rows = cache_k.shape[1], cache_k.shape[2]
    mem_heads = cache_mem_k.shape[3]

    w = {'w_in': jnp.swapaxes(w_in, 1, 2), 'w_branch_a': w_branch_a, 'w_branch_b': w_branch_b,
         'w_branch_c': w_branch_c, 'w_out': w_out, 'w_mq': w_mq, 'w_mo': w_mo, 'w_up': w_up, 'w_down': w_down}

    ck_pages = cache_k.reshape(depth, n_pool, page_rows * fox_heads, dh)
    cv_pages = cache_v.reshape(depth, n_pool, page_rows * fox_heads, dh)
    lft_pages = cache_logf.astype(F32).transpose(0, 1, 3, 2)

    lbs = lower_bounds(hgrn_lb_logits.astype(F32))
    xp = x_prompt.reshape(bp * tp, d)
    xs = x_sample.reshape(bs * ts, d)
    memp = mem_prompt.reshape(bp * n_mem, d)
    rows_p, rows_s, mem_rows = [], [], []
    for l in range(depth):
        p = {'norm_mix_g': norm_mix_g[l], 'hgrn_norm_g': hgrn_norm_g[l],
             'conv_dw_w': conv_dw_w[l], 'conv_dw_b': conv_dw_b[l], 'conv_ln_g': conv_ln_g[l],
             'conv_ln_b': conv_ln_b[l], 'fox_f_bias': fox_f_bias[l],
             'norm_mem_g': norm_mem_g[l], 'norm_ffn_g': norm_ffn_g[l],
             'ffn_dw_w': ffn_dw_w[l], 'ffn_dw_b': ffn_dw_b[l]}
        lb = lbs[l]

        mn = rmsnorm(memp, mem_kv_norm_g[l], BF16, 512)
        tm_m, tn_m = _proj_tiles(bp * n_mem, d)
        mk = matmul(mn, w_mk, l, 0, d, tm=tm_m, tn=tn_m)
        mv = matmul(mn, w_mv, l, 0, d, tm=tm_m, tn=tn_m)
        mem_rows.append((mk, mv))
        s0 = jnp.zeros((bp, hgrn_heads, LANES, LANES), F32)
        cb0 = jnp.zeros((bp, conv_dw_w.shape[1] - 1, d), F32)
        fb0 = jnp.zeros((bp, ffn_dw_w.shape[1] - 1, w_down.shape[1]), F32)
        attend_p = lambda cq, ck, cv, c_pad, c_t: fox_prompt(cq, ck, cv, c_pad, c_t, bp, tp, fox_heads, 256, 512, 2)
        xp, new_p = _trunk_layer(xp, bp, tp, l, w, p, lb, mk.reshape(bp, n_mem, d), mv.reshape(bp, n_mem, d),
                                 mem_heads, s0, cb0, fb0, attend_p, True)
        rows_p.append(new_p)

        def attend_s(cq, ck, cv, c_pad, c_t, l=l):
            c_col = jnp.broadcast_to(c_t.reshape(bs, fox_heads * ts, 1), (bs, fox_heads * ts, LANES))
            c_row = jnp.repeat(jnp.pad(c_t, ((0, 0), (0, 0), (0, LANES - ts))), ts, axis=1)
            return fox_sample(cq, ck, cv, c_col, c_row, ck_pages, cv_pages, lft_pages, l, page_table, fox_heads)

        xs, new_s = _trunk_layer(xs, bs, ts, l, w, p, lb, cache_mem_k[l].reshape(bs, n_mem, d),
                                 cache_mem_v[l].reshape(bs, n_mem, d), mem_heads, state_hgrn[l],
                                 state_conv[l], state_ffn[l], attend_s, False)
        rows_s.append(new_s)

    y_prompt = rmsnorm(xp, final_norm_g, F32, 512).reshape(bp, tp, d)
    y_sample = rmsnorm(xs, final_norm_g, F32, bs * ts).reshape(bs, ts, d)

    def stack(rows, i, shape):
        return jnp.stack([r[i].reshape(shape) for r in rows])

    k_prompt = stack(rows_p, 0, (bp, tp, fox_heads, dh))
    v_prompt = stack(rows_p, 1, (bp, tp, fox_heads, dh))
    logf_prompt = stack(rows_p, 2, (bp, tp, fox_heads))
    hgrn_prompt = stack(rows_p, 3, (bp, hgrn_heads, LANES, LANES))
    conv_prompt = stack(rows_p, 4, (bp, conv_dw_w.shape[1] - 1, d))
    ffn_prompt = stack(rows_p, 5, (bp, ffn_dw_w.shape[1] - 1, w_down.shape[1]))
    mem_k_prompt = stack(mem_rows, 0, (bp, n_mem, mem_heads, d // mem_heads))
    mem_v_prompt = stack(mem_rows, 1, (bp, n_mem, mem_heads, d // mem_heads))
    k_sample = stack(rows_s, 0, (bs, ts, fox_heads, dh))
    v_sample = stack(rows_s, 1, (bs, ts, fox_heads, dh))
    logf_sample = stack(rows_s, 2, (bs, ts, fox_heads))
    hgrn_sample = stack(rows_s, 3, (bs, hgrn_heads, LANES, LANES))
    conv_sample = stack(rows_s, 4, (bs, conv_dw_w.shape[1] - 1, d))
    ffn_sample = stack(rows_s, 5, (bs, ffn_dw_w.shape[1] - 1, w_down.shape[1]))
    return (y_prompt, y_sample, k_prompt, v_prompt, logf_prompt, mem_k_prompt, mem_v_prompt, hgrn_prompt,
            conv_prompt, ffn_prompt, k_sample, v_sample, logf_sample, hgrn_sample, conv_sample, ffn_sample)
```

```python
import functools

import jax
import jax.numpy as jnp
from jax import lax
from jax.experimental import pallas as pl
from jax.experimental.pallas import tpu as pltpu

F32 = jnp.float32
BF16 = jnp.bfloat16

EPS = 1e-6
LANES = 128
SUBLANES = 8
VMEM_LIMIT = 56 * 1024 * 1024

HGRN_CHUNK = 128
HGRN_SUB = 16
HGRN_HEAD_BLOCK = 8
CONV_HALO = 32
FFN_HALO = 8
FOX_PAGES_PER_STEP = 8


def _cparams(*sem):
    return pltpu.CompilerParams(dimension_semantics=sem, vmem_limit_bytes=VMEM_LIMIT)


def _dot(a, b):
    return jnp.dot(a, b, preferred_element_type=F32)


def _dot_nt(a, b):
    return lax.dot_general(a, b, (((1,), (1,)), ((), ())), preferred_element_type=F32)


def _sigmoid(x):
    return 1.0 / (1.0 + jnp.exp(-x))


def _log_sigmoid(x):
    return jnp.minimum(x, 0.0) - jnp.log(1.0 + jnp.exp(-jnp.abs(x)))


def _split3(x):
    hi = x.astype(BF16)
    r = x - hi.astype(F32)
    mid = r.astype(BF16)
    lo = (r - mid.astype(F32)).astype(BF16)
    return hi, mid, lo


def _dot_exact_left(m, x):
    hi, mid, lo = _split3(x)
    return _dot(m, hi) + _dot(m, mid) + _dot(m, lo)


def _dot_exact_right(x, m):
    hi, mid, lo = _split3(x)
    return _dot(hi, m) + _dot(mid, m) + _dot(lo, m)


def _rms_body(x_ref, g_ref, o_ref):
    x = x_ref[...]
    ms = jnp.mean(x * x, axis=-1, keepdims=True)
    o_ref[...] = (x * lax.rsqrt(ms + EPS) * g_ref[...]).astype(o_ref.dtype)


def rmsnorm(x, g, out_dtype, tm):
    n, d = x.shape
    return pl.pallas_call(
        _rms_body,
        out_shape=jax.ShapeDtypeStruct((n, d), out_dtype),
        grid=(n // tm,),
        in_specs=[pl.BlockSpec((tm, d), lambda i: (i, 0)), pl.BlockSpec((1, d), lambda i: (0, 0))],
        out_specs=pl.BlockSpec((tm, d), lambda i: (i, 0)),
        compiler_params=_cparams("arbitrary"),
        name="rmsnorm",
    )(x, g.reshape(1, d))


def _mm_body(*refs, has_res, w_t, shift):
    refs = list(refs)
    x_ref, w_ref = refs[0], refs[1]
    if shift:
        w2_ref, tail_ref = refs[2:4]
        del refs[2:4]
    if has_res:
        r_ref, o_ref, wb_ref = refs[2:]
    else:
        o_ref, wb_ref = refs[2:]

    @pl.when(pl.program_id(1) == 0)
    def _():
        wt = w_ref[...]
        if shift:
            last = pl.program_id(0) == pl.num_programs(0) - 1
            nxt = jnp.where(last, tail_ref[...], w2_ref[...])
            wt = jnp.concatenate([wt, nxt], axis=0)[shift:shift + wt.shape[0]]
        if w_t:
            wt = wt.T
        wb_ref[...] = wt.astype(BF16)

    acc = _dot(x_ref[...].astype(BF16), wb_ref[...])
    if has_res:
        acc = acc + r_ref[...]
    o_ref[...] = acc.astype(o_ref.dtype)


def matmul(x, w, layer, col0, ncols, *, tm, tn, out_dtype=F32, residual=None, w_t=False):
    n, k = x.shape
    shift = col0 % LANES if w_t else 0
    col0 -= shift
    assert w.shape[2 if w_t else 1] == k and col0 % tn == 0 and ncols % tn == 0 and n % tm == 0
    assert shift % SUBLANES == 0
    c0 = col0 // tn
    if w_t:
        w_spec = pl.BlockSpec((None, tn, k), lambda j, i: (layer, c0 + j, 0))
    else:
        w_spec = pl.BlockSpec((None, k, tn), lambda j, i: (layer, 0, c0 + j))
    in_specs = [pl.BlockSpec((tm, k), lambda j, i: (i, 0)), w_spec]
    args = [x, w]
    if shift:
        per = tn // LANES
        nj = ncols // tn
        end = col0 + ncols
        assert end + shift <= w.shape[1]
        tail = jnp.pad(w[layer, end:end + shift, :], ((0, LANES - shift), (0, 0)))
        in_specs.append(pl.BlockSpec((None, LANES, k),
                                     lambda j, i: (layer, (c0 + jnp.minimum(j + 1, nj - 1)) * per, 0)))
        in_specs.append(pl.BlockSpec((LANES, k), lambda j, i: (0, 0)))
        args += [w, tail]
    if residual is not None:
        in_specs.append(pl.BlockSpec((tm, tn), lambda j, i: (i, j)))
        args.append(residual)
    return pl.pallas_call(
        functools.partial(_mm_body, has_res=residual is not None, w_t=w_t, shift=shift),
        out_shape=jax.ShapeDtypeStruct((n, ncols), out_dtype),
        grid=(ncols // tn, n // tm),
        in_specs=in_specs,
        out_specs=pl.BlockSpec((tm, tn), lambda j, i: (i, j)),
        scratch_shapes=[pltpu.VMEM((k, tn), BF16)],
        compiler_params=_cparams("arbitrary", "arbitrary"),
        name="matmul",
    )(*args)


def _merge_body(oa_ref, ob_ref, oc_ref, wa_ref, wb_ref, wc_ref, ga_ref, gb_ref, gc_ref, o_ref,
                wa_s, wb_s, wc_s):
    @pl.when(pl.program_id(1) == 0)
    def _():
        wa_s[...] = wa_ref[...].astype(BF16)
        wb_s[...] = wb_ref[...].astype(BF16)
        wc_s[...] = wc_ref[...].astype(BF16)

    m = _sigmoid(ga_ref[...]) * _dot(oa_ref[...].astype(BF16), wa_s[...])
    m = m + _sigmoid(gb_ref[...]) * _dot(ob_ref[...].astype(BF16), wb_s[...])
    m = m + _sigmoid(gc_ref[...]) * _dot(oc_ref[...].astype(BF16), wc_s[...])
    o_ref[...] = m.astype(o_ref.dtype)


def branch_merge(oa, ob, oc, wa, wb, wc, layer, gate, *, tm, tn):
    n, d = oa.shape
    nb = d // tn
    act = pl.BlockSpec((tm, d), lambda j, i: (i, 0))
    wsp = pl.BlockSpec((None, d, tn), lambda j, i: (layer, 0, j))
    gsp = [pl.BlockSpec((tm, tn), lambda j, i, b=b: (i, b * nb + j)) for b in range(3)]
    return pl.pallas_call(
        _merge_body,
        out_shape=jax.ShapeDtypeStruct((n, d), BF16),
        grid=(nb, n // tm),
        in_specs=[act, act, act, wsp, wsp, wsp] + gsp,
        out_specs=pl.BlockSpec((tm, tn), lambda j, i: (i, j)),
        scratch_shapes=[pltpu.VMEM((d, tn), BF16)] * 3,
        compiler_params=_cparams("arbitrary", "arbitrary"),
        name="branch_merge",
    )(oa, ob, oc, wa, wb, wc, gate, gate, gate)


def _lb_body(x_ref, o_ref):
    x = x_ref[...]
    depth = x.shape[0]
    mx = jnp.max(x, axis=0, keepdims=True)
    e = jnp.exp(x - mx)
    soft = e / jnp.sum(e, axis=0, keepdims=True)
    run = jnp.zeros_like(soft[0:1])
    o_ref[0:1, :] = run
    for l in range(1, depth):
        run = run + soft[l:l + 1]
        o_ref[l:l + 1, :] = run


def lower_bounds(logits):
    return pl.pallas_call(
        _lb_body, out_shape=jax.ShapeDtypeStruct(logits.shape, F32), name="hgrn_lower_bounds",
    )(logits)


def _logf_body(h_ref, w_ref, b_ref, lf_ref, c_ref, carry):
    t = h_ref.shape[0]
    rows = max(t, LANES)

    @pl.when(pl.program_id(1) == 0)
    def _():
        carry[...] = jnp.zeros_like(carry)

    z = _dot_nt(h_ref[...].astype(BF16), w_ref[...].astype(BF16)) + b_ref[...]
    lf = _log_sigmoid(z)
    lf_ref[...] = lf
    if rows > t:
        lf = jnp.concatenate([lf, jnp.zeros((rows - t, LANES), F32)], axis=0)
    r = lax.broadcasted_iota(jnp.int32, (rows, rows), 0)
    s = lax.broadcasted_iota(jnp.int32, (rows, rows), 1)
    tril = jnp.where(s <= r, 1.0, 0.0).astype(BF16)
    c = _dot_exact_left(tril, lf) + carry[...]
    c_ref[...] = c[:t]
    carry[...] = c[rows - 1:rows]


def fox_logf(h, w_in, layer, col_block, bias_pad, b, t, tt):
    n, k = h.shape
    nt = t // tt
    return pl.pallas_call(
        _logf_body,
        out_shape=(jax.ShapeDtypeStruct((n, LANES), F32), jax.ShapeDtypeStruct((n, LANES), F32)),
        grid=(b, nt),
        in_specs=[pl.BlockSpec((tt, k), lambda i, j: (i * nt + j, 0)),
                  pl.BlockSpec((None, LANES, k), lambda i, j: (layer, col_block, 0)),
                  pl.BlockSpec((1, LANES), lambda i, j: (0, 0))],
        out_specs=(pl.BlockSpec((tt, LANES), lambda i, j: (i * nt + j, 0)),
                   pl.BlockSpec((tt, LANES), lambda i, j: (i * nt + j, 0))),
        scratch_shapes=[pltpu.VMEM((1, LANES), F32)],
        compiler_params=_cparams("arbitrary", "arbitrary"),
        name="fox_logf",
    )(h, w_in, bias_pad)


def _hgrn_body(aq_ref, af_ref, ai_ref, ag_ref, lb_ref, ng_ref, s0_ref, o_ref, s_ref,
               st, g_s, q_s, k_s, v_s, o_s, *, hb):
    L, C = HGRN_CHUNK, HGRN_SUB
    tb = aq_ref.shape[0]
    c = pl.program_id(2)

    @pl.when(c == 0)
    def _():
        for h in range(hb):
            st[h] = s0_ref[0, h].T

    r = lax.broadcasted_iota(jnp.int32, (L, L), 0)
    s = lax.broadcasted_iota(jnp.int32, (L, L), 1)
    tril = jnp.where(s <= r, 1.0, 0.0).astype(BF16)
    row = lax.broadcasted_iota(jnp.int32, (L, 1), 0)

    for h in range(hb):
        ls = slice(h * LANES, (h + 1) * LANES)
        lb = lb_ref[:, ls]
        qr = aq_ref[:, ls]
        q = qr * _sigmoid(qr)
        fg = lb + (1.0 - lb) * _sigmoid(af_ref[:, ls])
        kk = 1.0 - fg
        gl = jnp.log(fg)
        v = ai_ref[:, ls]
        if tb < L:
            pad = jnp.zeros((L - tb, LANES), F32)
            q, kk, gl, v = (jnp.concatenate([a, pad], axis=0) for a in (q, kk, gl, v))

        G = _dot_exact_left(tril, gl)
        st_old = st[h]

        o = _dot_nt((q * jnp.exp(G)).astype(BF16), st_old.astype(BF16))

        a_off = jnp.zeros((L, L), F32)
        blk = C
        while blk < tb:
            grp = 2 * blk
            gref = jnp.concatenate(
                [jnp.broadcast_to(G[g0 + blk - 1:g0 + blk, :], (grp, LANES)) for g0 in range(0, L, grp)],
                axis=0)
            upper = (row // blk) % 2 == 1
            qt = q * jnp.exp(jnp.where(upper, G - gref, -jnp.inf))
            kt = kk * jnp.exp(jnp.where(upper, -jnp.inf, gref - G))
            a = _dot_nt(qt.astype(BF16), kt.astype(BF16))
            a_off = a_off + jnp.where(r // grp == s // grp, a, 0.0)
            blk = grp
        o_s[h] = o + _dot(a_off.astype(BF16), v.astype(BF16)) if tb > C else o

        g_s[h] = G
        q_s[h] = q
        k_s[h] = kk
        v_s[h] = v

        gl_last = G[L - 1:L, :]
        k_dec = kk * jnp.exp(gl_last - G)
        st[h] = st_old * jnp.exp(gl_last) + _dot(v.T.astype(BF16), k_dec.astype(BF16))

    ones = jnp.ones((LANES, LANES), BF16)
    rowc = lax.broadcasted_iota(jnp.int32, (C, LANES), 0)

    def sub_block(j, carry):
        base = pl.multiple_of(j * C, C)
        for h in range(hb):
            gj = g_s[h, pl.ds(base, C), :]
            qj = q_s[h, pl.ds(base, C), :]
            parts = []
            for si in range(C):
                gs = g_s[h, pl.ds(base + si, 1), :]
                ks = k_s[h, pl.ds(base + si, 1), :]
                e = jnp.exp(jnp.where(rowc >= si, gj - gs, -jnp.inf))
                parts.append(qj * e * ks)
            rs = _dot(jnp.concatenate(parts, axis=0).astype(BF16), ones)
            acc = o_s[h, pl.ds(base, C), :]
            for si in range(C):
                acc = acc + rs[si * C:(si + 1) * C] * v_s[h, pl.ds(base + si, 1), :]
            o_s[h, pl.ds(base, C), :] = acc
        return carry

    lax.fori_loop(0, -(-tb // C), sub_block, 0)

    for h in range(hb):
        ls = slice(h * LANES, (h + 1) * LANES)
        o = o_s[h][:tb]
        ms = jnp.mean(o * o, axis=-1, keepdims=True)
        o = o * lax.rsqrt(ms + EPS) * ng_ref[...] * _sigmoid(ag_ref[:, ls])
        o_ref[:, ls] = o.astype(o_ref.dtype)

    @pl.when(c == pl.num_programs(2) - 1)
    def _():
        for h in range(hb):
            s_ref[0, h] = st[h].T


def hgrn2(za, lb, norm_g, s0, b, t, out_dtype):
    n = za.shape[0]
    heads = s0.shape[1]
    assert s0.shape[2:] == (LANES, LANES)
    hb = min(HGRN_HEAD_BLOCK, heads)
    nhb = heads // hb
    tb = min(t, HGRN_CHUNK)
    nc = t // tb
    w = hb * LANES
    col = lambda part: pl.BlockSpec((tb, w), lambda i, h, c, part=part: (i * nc + c, part * nhb + h))
    return pl.pallas_call(
        functools.partial(_hgrn_body, hb=hb),
        out_shape=(jax.ShapeDtypeStruct((n, heads * LANES), out_dtype),
                   jax.ShapeDtypeStruct(s0.shape, F32)),
        grid=(b, nhb, nc),
        in_specs=[col(0), col(1), col(2), col(3),
                  pl.BlockSpec((1, w), lambda i, h, c: (0, h)),
                  pl.BlockSpec((1, LANES), lambda i, h, c: (0, 0)),
                  pl.BlockSpec((1, hb, LANES, LANES), lambda i, h, c: (i, h, 0, 0))],
        out_specs=(pl.BlockSpec((tb, w), lambda i, h, c: (i * nc + c, h)),
                   pl.BlockSpec((1, hb, LANES, LANES), lambda i, h, c: (i, h, 0, 0))),
        scratch_shapes=[pltpu.VMEM((hb, LANES, LANES), F32)] + [pltpu.VMEM((hb, HGRN_CHUNK, LANES), F32)] * 5,
        compiler_params=_cparams("arbitrary", "arbitrary", "arbitrary"),
        name="hgrn2",
    )(za, za, za, za, lb.reshape(1, -1), norm_g.reshape(1, -1), s0)


def _conv_body(x1_ref, x2_ref, buf_ref, w_ref, b_ref, lg_ref, lbias_ref, o_ref, new_ref, u_s, y_s, sh_s):
    tt, ch = x1_ref.shape
    width = w_ref.shape[0]
    hist = width - 1
    i = pl.program_id(1)

    @pl.when(i == 0)
    def _():
        u_s[0:CONV_HALO - hist, :] = jnp.zeros((CONV_HALO - hist, ch), F32)
        u_s[CONV_HALO - hist:CONV_HALO, :] = buf_ref[0]

    u_s[CONV_HALO:CONV_HALO + tt, :] = x1_ref[...] * _sigmoid(x2_ref[...])

    def lane_block(cb, carry):
        ls = pl.ds(pl.multiple_of(cb * LANES, LANES), LANES)
        acc = jnp.broadcast_to(b_ref[:, ls], (tt, LANES))
        for s in range(SUBLANES):
            taps = [j for j in range(width) if (CONV_HALO - hist + j) % SUBLANES == s]
            if not taps:
                continue
            span = max(CONV_HALO - hist + j - s for j in taps) + tt
            if s:
                sh_s[s, 0:span, :] = u_s[pl.ds(s, span), ls]
            for j in taps:
                off = CONV_HALO - hist + j - s
                xs = sh_s[s, off:off + tt, :] if s else u_s[off:off + tt, ls]
                acc = acc + w_ref[j:j + 1, ls] * xs
        y_s[:, ls] = acc
        return carry

    lax.fori_loop(0, ch // LANES, lane_block, 0)

    y = y_s[...]
    mu = jnp.mean(y, axis=-1, keepdims=True)
    yc = y - mu
    var = jnp.mean(yc * yc, axis=-1, keepdims=True)
    yn = yc * lax.rsqrt(var + EPS) * lg_ref[...] + lbias_ref[...]
    o_ref[...] = (yn * _sigmoid(yn)).astype(o_ref.dtype)

    @pl.when(i == pl.num_programs(1) - 1)
    def _():
        new_ref[0] = u_s[CONV_HALO + tt - hist:CONV_HALO + tt, :]

    if tt >= CONV_HALO:
        @pl.when(i < pl.num_programs(1) - 1)
        def _():
            u_s[0:CONV_HALO, :] = u_s[tt:tt + CONV_HALO, :]


def conv_branch(zb, buf, w, bias, ln_g, ln_b, b, t, tt, out_dtype):
    n = zb.shape[0]
    ch = zb.shape[1] // 2
    width = w.shape[0]
    nt = t // tt
    assert nt == 1 or tt >= CONV_HALO
    vec = pl.BlockSpec((1, ch), lambda i, j: (0, 0))
    return pl.pallas_call(
        _conv_body,
        out_shape=(jax.ShapeDtypeStruct((n, ch), out_dtype),
                   jax.ShapeDtypeStruct((b, width - 1, ch), F32)),
        grid=(b, nt),
        in_specs=[pl.BlockSpec((tt, ch), lambda i, j: (i * nt + j, 0)),
                  pl.BlockSpec((tt, ch), lambda i, j: (i * nt + j, 1)),
                  pl.BlockSpec((1, width - 1, ch), lambda i, j: (i, 0, 0)),
                  pl.BlockSpec((width, ch), lambda i, j: (0, 0)),
                  vec, vec, vec],
        out_specs=(pl.BlockSpec((tt, ch), lambda i, j: (i * nt + j, 0)),
                   pl.BlockSpec((1, width - 1, ch), lambda i, j: (i, 0, 0))),
        scratch_shapes=[pltpu.VMEM((CONV_HALO + tt, ch), F32), pltpu.VMEM((tt, ch), F32),
                        pltpu.VMEM((SUBLANES, CONV_HALO + tt, LANES), F32)],
        compiler_params=_cparams("arbitrary", "arbitrary"),
        name="conv_branch",
    )(zb, zb, buf, w, bias.reshape(1, ch), ln_g.reshape(1, ch), ln_b.reshape(1, ch))


def _fox_prompt_body(q_ref, k_ref, v_ref, c_ref, ct_ref, o_ref, *, tk, hb, dh):
    tq = q_ref.shape[0]
    scale = dh ** -0.5
    hp = pl.program_id(1)
    i = pl.program_id(2)
    lane = lax.broadcasted_iota(jnp.int32, (tq, LANES), 1)
    c_blk = c_ref[...]
    qs, cqs = [], []
    for hh in range(hb):
        qs.append((q_ref[:, hh * dh:(hh + 1) * dh] * scale).astype(BF16))
        cqs.append(jnp.sum(jnp.where(lane == hp * hb + hh, c_blk, 0.0), axis=1, keepdims=True))

    def kv_block(j, carry, masked):
        start = pl.multiple_of(j * tk, tk)
        out = []
        for hh in range(hb):
            m, l, acc = carry[hh]
            hs = slice(hh * dh, (hh + 1) * dh)
            ks = k_ref[pl.ds(start, tk), hs].astype(BF16)
            vs = v_ref[pl.ds(start, tk), hs].astype(BF16)
            ck = ct_ref[0, pl.ds(hp * hb + hh, 1), pl.ds(start, tk)]
            s = _dot_nt(qs[hh], ks) + cqs[hh] - ck
            if masked:
                qpos = i * tq + lax.broadcasted_iota(jnp.int32, (tq, tk), 0)
                kpos = j * tk + lax.broadcasted_iota(jnp.int32, (tq, tk), 1)
                s = jnp.where(kpos <= qpos, s, -jnp.inf)
            m_new = jnp.maximum(m, jnp.max(s, axis=1, keepdims=True))
            p = jnp.exp(s - m_new)
            alpha = jnp.exp(m - m_new)
            l = alpha * l + jnp.sum(p, axis=1, keepdims=True)
            acc = alpha * acc + _dot(p.astype(BF16), vs)
            out.append((m_new, l, acc))
        return tuple(out)

    n_full = (i * tq) // tk
    init = tuple((jnp.full((tq, 1), -jnp.inf, F32), jnp.zeros((tq, 1), F32), jnp.zeros((tq, dh), F32))
                 for _ in range(hb))
    carry = lax.fori_loop(0, n_full, lambda j, c: kv_block(j, c, False), init)
    carry = kv_block(n_full, carry, True)
    for hh in range(hb):
        _, l, acc = carry[hh]
        o_ref[:, hh * dh:(hh + 1) * dh] = (acc / l).astype(o_ref.dtype)


def fox_prompt(cq, ck, cv, c_pad, c_t, b, t, heads, tq, tk, hb):
    n = cq.shape[0]
    dh = cq.shape[1] // heads
    nq = t // tq
    assert tk % tq == 0 and t % tk == 0 and heads % hb == 0
    w = hb * dh
    return pl.pallas_call(
        functools.partial(_fox_prompt_body, tk=tk, hb=hb, dh=dh),
        out_shape=jax.ShapeDtypeStruct((n, heads * dh), BF16),
        grid=(b, heads // hb, nq),
        in_specs=[pl.BlockSpec((tq, w), lambda bi, h, i: (bi * nq + i, h)),
                  pl.BlockSpec((t, w), lambda bi, h, i: (bi, h)),
                  pl.BlockSpec((t, w), lambda bi, h, i: (bi, h)),
                  pl.BlockSpec((tq, LANES), lambda bi, h, i: (bi * nq + i, 0)),
                  pl.BlockSpec((1, c_t.shape[1], t), lambda bi, h, i: (bi, 0, 0))],
        out_specs=pl.BlockSpec((tq, w), lambda bi, h, i: (bi * nq + i, h)),
        compiler_params=_cparams("arbitrary", "arbitrary", "arbitrary"),
        name="fox_prompt",
    )(cq, ck, cv, c_pad, c_t)


def _fox_sample_body(pt_ref, q_ref, kn_ref, vn_ref, ccol_ref, crow_ref, *rest, heads, gp):
    kp_refs, vp_refs, lft_refs = rest[:gp], rest[gp:2 * gp], rest[2 * gp:3 * gp]
    o_ref, m_s, l_s, acc_s, carry_s = rest[3 * gp:]
    t, d = q_ref.shape
    dh = d // heads
    rows = lft_refs[0].shape[1]
    scale = dh ** -0.5
    p = pl.program_id(1)
    qb = [q_ref[:, h * dh:(h + 1) * dh].astype(BF16) for h in range(heads)]
    ccol = ccol_ref[0]

    def update(s, values_of):
        m_old = m_s[...]
        m_new = jnp.maximum(m_old, jnp.max(s, axis=1, keepdims=True))
        pe = jnp.exp(s - m_new)
        alpha = jnp.exp(m_old - m_new)
        l_s[...] = alpha * l_s[...] + jnp.sum(pe, axis=1, keepdims=True)
        pv = [_dot(pe[h * t:(h + 1) * t].astype(BF16), values_of(h)) for h in range(heads)]
        acc_s[...] = alpha * acc_s[...] + jnp.concatenate(pv, axis=0)
        m_s[...] = m_new

    @pl.when(p == 0)
    def _():
        carry_s[...] = jnp.zeros_like(carry_s)
        m_s[...] = jnp.full_like(m_s, -jnp.inf)
        l_s[...] = jnp.zeros_like(l_s)
        acc_s[...] = jnp.zeros_like(acc_s)
        pad = jnp.zeros((rows - t, dh), F32)
        sc = []
        for h in range(heads):
            kn = jnp.concatenate([kn_ref[:, h * dh:(h + 1) * dh], pad], axis=0)
            sc.append(_dot_nt(qb[h], kn.astype(BF16)) * scale)
        s = jnp.concatenate(sc, axis=0) + ccol - crow_ref[0]
        r = lax.broadcasted_iota(jnp.int32, (heads * t, rows), 0)
        c = lax.broadcasted_iota(jnp.int32, (heads * t, rows), 1)
        s = jnp.where(c <= r % t, s, -jnp.inf)
        update(s, lambda h: jnp.concatenate([vn_ref[:, h * dh:(h + 1) * dh], pad], axis=0).astype(BF16))

    rr = lax.broadcasted_iota(jnp.int32, (rows, rows), 0)
    cc = lax.broadcasted_iota(jnp.int32, (rows, rows), 1)
    later = jnp.where(rr > cc, 1.0, 0.0).astype(BF16)
    for g in range(gp):
        kp_ref, vp_ref = kp_refs[g], vp_refs[g]
        lft = lft_refs[g][...]
        d_t = _dot_exact_right(lft, later) + carry_s[...]
        carry_s[...] = carry_s[...] + jnp.sum(lft, axis=1, keepdims=True)
        sc = []
        for h in range(heads):
            kh = kp_ref[pl.ds(h, rows, stride=heads), :].astype(BF16)
            sc.append(_dot_nt(qb[h], kh) * scale + d_t[h:h + 1, :])
        s = jnp.concatenate(sc, axis=0) + ccol
        update(s, lambda h: vp_ref[pl.ds(h, rows, stride=heads), :].astype(BF16))

    @pl.when(p == pl.num_programs(1) - 1)
    def _():
        o = acc_s[...] / l_s[...]
        for h in range(heads):
            o_ref[:, h * dh:(h + 1) * dh] = o[h * t:(h + 1) * t]


def fox_sample(cq, ck_new, cv_new, c_col, c_row, cache_k, cache_v, cache_lft, layer, page_table, heads):
    n, d = cq.shape
    b, n_pages = page_table.shape
    t = n // b
    dh = d // heads
    rows = cache_lft.shape[3]
    gp = FOX_PAGES_PER_STEP
    assert rows == LANES and dh == LANES and n_pages % gp == 0
    tok = lambda bi, p, pt: (bi, 0)
    tab = lambda bi, p, pt: (bi, 0, 0)

    def page(g):
        return lambda bi, p, pt: (layer, pt[bi, n_pages - 1 - (gp * p + g)], 0, 0)

    kv_specs = [pl.BlockSpec((None, None, rows * heads, dh), page(g)) for g in range(gp)]
    lf_specs = [pl.BlockSpec((None, None, heads, rows), page(g)) for g in range(gp)]
    grid_spec = pltpu.PrefetchScalarGridSpec(
        num_scalar_prefetch=1,
        grid=(b, n_pages // gp),
        in_specs=[pl.BlockSpec((t, d), tok), pl.BlockSpec((t, d), tok), pl.BlockSpec((t, d), tok),
                  pl.BlockSpec((1, heads * t, LANES), tab), pl.BlockSpec((1, heads * t, LANES), tab)]
        + kv_specs + kv_specs + lf_specs,
        out_specs=pl.BlockSpec((t, d), tok),
        scratch_shapes=[pltpu.VMEM((heads * t, LANES), F32), pltpu.VMEM((heads * t, LANES), F32),
                        pltpu.VMEM((heads * t, dh), F32), pltpu.VMEM((heads, LANES), F32)],
    )
    return pl.pallas_call(
        functools.partial(_fox_sample_body, heads=heads, gp=gp),
        out_shape=jax.ShapeDtypeStruct((n, d), F32),
        grid_spec=grid_spec,
        compiler_params=_cparams("arbitrary", "arbitrary"),
        name="fox_sample",
    )(page_table, cq, ck_new, cv_new, c_col, c_row, *([cache_k] * gp), *([cache_v] * gp), *([cache_lft] * gp))


def _mem_attn_body(q_ref, k_ref, v_ref, o_ref, *, heads):
    d = q_ref.shape[1]
    dh = d // heads
    scale = dh ** -0.5
    for h in range(heads):
        hs = slice(h * dh, (h + 1) * dh)
        s = _dot_nt(q_ref[:, hs].astype(BF16), k_ref[0, :, hs].astype(BF16)) * scale
        m = jnp.max(s, axis=1, keepdims=True)
        pe = jnp.exp(s - m)
        o = _dot(pe.astype(BF16), v_ref[0, :, hs].astype(BF16)) / jnp.sum(pe, axis=1, keepdims=True)
        o_ref[:, hs] = o.astype(o_ref.dtype)


def mem_attention(qm, mem_k, mem_v, b, t, tt, heads, out_dtype):
    n, d = qm.shape
    nm = mem_k.shape[1]
    nt = t // tt
    return pl.pallas_call(
        functools.partial(_mem_attn_body, heads=heads),
        out_shape=jax.ShapeDtypeStruct((n, d), out_dtype),
        grid=(b, nt),
        in_specs=[pl.BlockSpec((tt, d), lambda i, j: (i * nt + j, 0)),
                  pl.BlockSpec((1, nm, d), lambda i, j: (i, 0, 0)),
                  pl.BlockSpec((1, nm, d), lambda i, j: (i, 0, 0))],
        out_specs=pl.BlockSpec((tt, d), lambda i, j: (i * nt + j, 0)),
        compiler_params=_cparams("arbitrary", "arbitrary"),
        name="mem_attention",
    )(qm, mem_k, mem_v)


def _ffn_act_body(g_ref, v_ref, buf_ref, w_ref, b_ref, o_ref, new_ref, g_s, sh_s):
    t, tc = g_ref.shape
    width = w_ref.shape[0]
    hist = width - 1
    g_s[FFN_HALO - hist:FFN_HALO, :] = buf_ref[0]
    g_s[FFN_HALO:FFN_HALO + t, :] = g_ref[...]
    acc = jnp.broadcast_to(b_ref[...], (t, tc))
    for j in range(width):
        start = FFN_HALO - hist + j
        if start % SUBLANES:
            sh_s[j] = g_s[pl.ds(start, t), :]
            xs = sh_s[j]
        else:
            xs = g_s[start:start + t, :]
        acc = acc + w_ref[j:j + 1, :] * xs
    o_ref[...] = (acc * _sigmoid(acc) * v_ref[...]).astype(o_ref.dtype)
    new_ref[0] = g_s[FFN_HALO + t - hist:FFN_HALO + t, :]


def ffn_act(up, buf, w, bias, b, t, tc, out_dtype):
    n = up.shape[0]
    dff = up.shape[1] // 2
    width = w.shape[0]
    nj = dff // tc
    return pl.pallas_call(
        _ffn_act_body,
        out_shape=(jax.ShapeDtypeStruct((n, dff), out_dtype),
                   jax.ShapeDtypeStruct((b, width - 1, dff), F32)),
        grid=(b, nj),
        in_specs=[pl.BlockSpec((t, tc), lambda i, j: (i, j)),
                  pl.BlockSpec((t, tc), lambda i, j: (i, nj + j)),
                  pl.BlockSpec((1, width - 1, tc), lambda i, j: (i, 0, j)),
                  pl.BlockSpec((width, tc), lambda i, j: (0, j)),
                  pl.BlockSpec((1, tc), lambda i, j: (0, j))],
        out_specs=(pl.BlockSpec((t, tc), lambda i, j: (i, j)),
                   pl.BlockSpec((1, width - 1, tc), lambda i, j: (i, 0, j))),
        scratch_shapes=[pltpu.VMEM((FFN_HALO + t, tc), F32), pltpu.VMEM((width, t, tc), F32)],
        compiler_params=_cparams("arbitrary", "arbitrary"),
        name="ffn_act",
    )(up, up, buf, w, bias.reshape(1, dff))


def _proj_tiles(n, k):
    if k <= 2048:
        return min(n, 1024), 1024
    return min(n, 512), 512


def _trunk_layer(x, b, t, l, w, p, lb, mem_k, mem_v, mem_heads, s0, conv_buf, ffn_buf, attend, prompt):
    n, d = x.shape
    heads_a = s0.shape[1]
    wa_cols = heads_a * LANES
    act = BF16 if prompt else F32
    tm, tn = _proj_tiles(n, d)
    w_in = w['w_in']
    off_b = 4 * wa_cols
    off_c = off_b + 2 * d
    off_f = off_c + 3 * d
    fox_heads = p['fox_f_bias'].shape[0]

    h1 = rmsnorm(x, p['norm_mix_g'], act, min(n, 512))
    za = matmul(h1, w_in, l, 0, off_b, tm=tm, tn=tn, w_t=True)
    zb = matmul(h1, w_in, l, off_b, 2 * d, tm=tm, tn=tn, w_t=True)
    cq = matmul(h1, w_in, l, off_c, d, tm=tm, tn=tn, w_t=True)
    ck = matmul(h1, w_in, l, off_c + d, d, tm=tm, tn=tn, w_t=True)
    cv = matmul(h1, w_in, l, off_c + 2 * d, d, tm=tm, tn=tn, w_t=True)
    gate = matmul(h1, w_in, l, off_f + fox_heads, 3 * d, tm=tm, tn=tn, w_t=True)
    bias_pad = jnp.zeros((1, LANES), F32).at[0, :fox_heads].set(p['fox_f_bias'])
    logf_pad, c_pad = fox_logf(h1, w_in, l, off_f // LANES, bias_pad, b, t, min(t, 256))

    oa, s_new = hgrn2(za, lb, p['hgrn_norm_g'], s0, b, t, act)
    ob, conv_new = conv_branch(zb, conv_buf, p['conv_dw_w'], p['conv_dw_b'], p['conv_ln_g'], p['conv_ln_b'],
                               b, t, min(t, 256), act)
    c_t = c_pad[:, :fox_heads].reshape(b, t, fox_heads).transpose(0, 2, 1)
    oc = attend(cq, ck, cv, c_pad, c_t)

    m = branch_merge(oa, ob, oc, w['w_branch_a'], w['w_branch_b'], w['w_branch_c'], l, gate,
                     tm=min(n, 256), tn=512)
    x = matmul(m, w['w_out'], l, 0, d, tm=tm, tn=tn, residual=x)

    h2 = rmsnorm(x, p['norm_mem_g'], act, min(n, 512))
    qm = matmul(h2, w['w_mq'], l, 0, d, tm=tm, tn=tn, out_dtype=act)
    om = mem_attention(qm, mem_k, mem_v, b, t, min(t, 512), mem_heads, act)
    x = matmul(om, w['w_mo'], l, 0, d, tm=tm, tn=tn, residual=x)

    h3 = rmsnorm(x, p['norm_ffn_g'], act, min(n, 512))
    dff = w['w_down'].shape[1]
    up = matmul(h3, w['w_up'], l, 0, 2 * dff, tm=tm, tn=tn)
    a, ffn_new = ffn_act(up, ffn_buf, p['ffn_dw_w'], p['ffn_dw_b'], b, t, 512, act)
    tm_d, tn_d = _proj_tiles(n, dff)
    x = matmul(a, w['w_down'], l, 0, d, tm=tm_d, tn=tn_d, residual=x)

    logf = logf_pad[:, :fox_heads].reshape(b, t, fox_heads)
    return x, (ck, cv, logf, s_new, conv_new, ffn_new)


def kernel(x_prompt, x_sample, cache_k, cache_v, cache_logf, page_table, cache_mem_k, cache_mem_v, state_hgrn, state_conv, state_ffn, mem_prompt, norm_mix_g, w_in, hgrn_lb_logits, hgrn_norm_g, conv_dw_w, conv_dw_b, conv_ln_g, conv_ln_b, fox_f_bias, w_branch_a, w_branch_b, w_branch_c, w_out, norm_mem_g, mem_kv_norm_g, w_mq, w_mk, w_mv, w_mo, norm_ffn_g, w_up, ffn_dw_w, ffn_dw_b, w_down, final_norm_g):
    depth = w_in.shape[0]
    bp, tp, d = x_prompt.shape
    bs, ts, _ = x_sample.shape
    n_mem = mem_prompt.shape[1]
    fox_heads = fox_f_bias.shape[1]
    dh = d // fox_heads
    hgrn_heads = state_hgrn.shape[2]
    n_pool, page_rows = cache_k.shape[1], cache_k.shape[2]
    mem_heads = cache_mem_k.shape[3]

    w = {'w_in': jnp.swapaxes(w_in, 1, 2), 'w_branch_a': w_branch_a, 'w_branch_b': w_branch_b,
         'w_branch_c': w_branch_c, 'w_out': w_out, 'w_mq': w_mq, 'w_mo': w_mo, 'w_up': w_up, 'w_down': w_down}

    ck_pages = cache_k.reshape(depth, n_pool, page_rows * fox_heads, dh)
    cv_pages = cache_v.reshape(depth, n_pool, page_rows * fox_heads, dh)
    lft_pages = cache_logf.astype(F32).transpose(0, 1, 3, 2)

    lbs = lower_bounds(hgrn_lb_logits.astype(F32))
    xp = x_prompt.reshape(bp * tp, d)
    xs = x_sample.reshape(bs * ts, d)
    memp = mem_prompt.reshape(bp * n_mem, d)
    rows_p, rows_s, mem_rows = [], [], []
    for l in range(depth):
        p = {'norm_mix_g': norm_mix_g[l], 'hgrn_norm_g': hgrn_norm_g[l],
             'conv_dw_w': conv_dw_w[l], 'conv_dw_b': conv_dw_b[l], 'conv_ln_g': conv_ln_g[l],
             'conv_ln_b': conv_ln_b[l], 'fox_f_bias': fox_f_bias[l],
             'norm_mem_g': norm_mem_g[l], 'norm_ffn_g': norm_ffn_g[l],
             'ffn_dw_w': ffn_dw_w[l], 'ffn_dw_b': ffn_dw_b[l]}
        lb = lbs[l]

        mn = rmsnorm(memp, mem_kv_norm_g[l], BF16, 512)
        tm_m, tn_m = _proj_tiles(bp * n_mem, d)
        mk = matmul(mn, w_mk, l, 0, d, tm=tm_m, tn=tn_m)
        mv = matmul(mn, w_mv, l, 0, d, tm=tm_m, tn=tn_m)
        mem_rows.append((mk, mv))
        s0 = jnp.zeros((bp, hgrn_heads, LANES, LANES), F32)
        cb0 = jnp.zeros((bp, conv_dw_w.shape[1] - 1, d), F32)
        fb0 = jnp.zeros((bp, ffn_dw_w.shape[1] - 1, w_down.shape[1]), F32)
        attend_p = lambda cq, ck, cv, c_pad, c_t: fox_prompt(cq, ck, cv, c_pad, c_t, bp, tp, fox_heads, 256, 512, 4)
        xp, new_p = _trunk_layer(xp, bp, tp, l, w, p, lb, mk.reshape(bp, n_mem, d), mv.reshape(bp, n_mem, d),
                                 mem_heads, s0, cb0, fb0, attend_p, True)
        rows_p.append(new_p)

        def attend_s(cq, ck, cv, c_pad, c_t, l=l):
            c_col = jnp.broadcast_to(c_t.reshape(bs, fox_heads * ts, 1), (bs, fox_heads * ts, LANES))
            c_row = jnp.repeat(jnp.pad(c_t, ((0, 0), (0, 0), (0, LANES - ts))), ts, axis=1)
            return fox_sample(cq, ck, cv, c_col, c_row, ck_pages, cv_pages, lft_pages, l, page_table, fox_heads)

        xs, new_s = _trunk_layer(xs, bs, ts, l, w, p, lb, cache_mem_k[l].reshape(bs, n_mem, d),
                                 cache_mem_v[l].reshape(bs, n_mem, d), mem_heads, state_hgrn[l],
                                 state_conv[l], state_ffn[l], attend_s, False)
        rows_s.append(new_s)

    y_prompt = rmsnorm(xp, final_norm_g, F32, 512).reshape(bp, tp, d)
    y_sample = rmsnorm(xs, final_norm_g, F32, bs * ts).reshape(bs, ts, d)

    def stack(rows, i, shape):
        return jnp.stack([r[i].reshape(shape) for r in rows])

    k_prompt = stack(rows_p, 0, (bp, tp, fox_heads, dh))
    v_prompt = stack(rows_p, 1, (bp, tp, fox_heads, dh))
    logf_prompt = stack(rows_p, 2, (bp, tp, fox_heads))
    hgrn_prompt = stack(rows_p, 3, (bp, hgrn_heads, LANES, LANES))
    conv_prompt = stack(rows_p, 4, (bp, conv_dw_w.shape[1] - 1, d))
    ffn_prompt = stack(rows_p, 5, (bp, ffn_dw_w.shape[1] - 1, w_down.shape[1]))
    mem_k_prompt = stack(mem_rows, 0, (bp, n_mem, mem_heads, d // mem_heads))
    mem_v_prompt = stack(mem_rows, 1, (bp, n_mem, mem_heads, d // mem_heads))
    k_sample = stack(rows_s, 0, (bs, ts, fox_heads, dh))
    v_sample = stack(rows_s, 1, (bs, ts, fox_heads, dh))
    logf_sample = stack(rows_s, 2, (bs, ts, fox_heads))
    hgrn_sample = stack(rows_s, 3, (bs, hgrn_heads, LANES, LANES))
    conv_sample = stack(rows_s, 4, (bs, conv_dw_w.shape[1] - 1, d))
    ffn_sample = stack(rows_s, 5, (bs, ffn_dw_w.shape[1] - 1, w_down.shape[1]))
    return (y_prompt, y_sample, k_prompt, v_prompt, logf_prompt, mem_k_prompt, mem_v_prompt, hgrn_prompt,
            conv_prompt, ffn_prompt, k_sample, v_sample, logf_sample, hgrn_sample, conv_sample, ffn_sample)
```

```python
import functools

import jax
import jax.numpy as jnp
from jax import lax
from jax.experimental import pallas as pl
from jax.experimental.pallas import tpu as pltpu

F32 = jnp.float32
BF16 = jnp.bfloat16

EPS = 1e-6
LANES = 128
SUBLANES = 8
VMEM_LIMIT = 56 * 1024 * 1024

HGRN_CHUNK = 128
HGRN_SUB = 16
HGRN_HEAD_BLOCK = 16
CONV_HALO = 32
FFN_HALO = 8
FOX_PAGES_PER_STEP = 8


def _cparams(*sem):
    return pltpu.CompilerParams(dimension_semantics=sem, vmem_limit_bytes=VMEM_LIMIT)


def _dot(a, b):
    return jnp.dot(a, b, preferred_element_type=F32)


def _dot_nt(a, b):
    return lax.dot_general(a, b, (((1,), (1,)), ((), ())), preferred_element_type=F32)


def _sigmoid(x):
    return 1.0 / (1.0 + jnp.exp(-x))


def _log_sigmoid(x):
    return jnp.minimum(x, 0.0) - jnp.log(1.0 + jnp.exp(-jnp.abs(x)))


def _split3(x):
    hi = x.astype(BF16)
    r = x - hi.astype(F32)
    mid = r.astype(BF16)
    lo = (r - mid.astype(F32)).astype(BF16)
    return hi, mid, lo


def _dot_exact_left(m, x):
    hi, mid, lo = _split3(x)
    return _dot(m, hi) + _dot(m, mid) + _dot(m, lo)


def _dot_exact_right(x, m):
    hi, mid, lo = _split3(x)
    return _dot(hi, m) + _dot(mid, m) + _dot(lo, m)


def _rms_body(x_ref, g_ref, o_ref):
    x = x_ref[...]
    ms = jnp.mean(x * x, axis=-1, keepdims=True)
    o_ref[...] = (x * lax.rsqrt(ms + EPS) * g_ref[...]).astype(o_ref.dtype)


def rmsnorm(x, g, out_dtype, tm):
    n, d = x.shape
    return pl.pallas_call(
        _rms_body,
        out_shape=jax.ShapeDtypeStruct((n, d), out_dtype),
        grid=(n // tm,),
        in_specs=[pl.BlockSpec((tm, d), lambda i: (i, 0)), pl.BlockSpec((1, d), lambda i: (0, 0))],
        out_specs=pl.BlockSpec((tm, d), lambda i: (i, 0)),
        compiler_params=_cparams("arbitrary"),
        name="rmsnorm",
    )(x, g.reshape(1, d))


def _mm_body(*refs, has_res, w_t, shift):
    refs = list(refs)
    x_ref, w_ref = refs[0], refs[1]
    if shift:
        w2_ref, tail_ref = refs[2:4]
        del refs[2:4]
    if has_res:
        r_ref, o_ref, wb_ref = refs[2:]
    else:
        o_ref, wb_ref = refs[2:]

    @pl.when(pl.program_id(1) == 0)
    def _():
        wt = w_ref[...]
        if shift:
            last = pl.program_id(0) == pl.num_programs(0) - 1
            nxt = jnp.where(last, tail_ref[...], w2_ref[...])
            wt = jnp.concatenate([wt, nxt], axis=0)[shift:shift + wt.shape[0]]
        if w_t:
            wt = wt.T
        wb_ref[...] = wt.astype(BF16)

    acc = _dot(x_ref[...].astype(BF16), wb_ref[...])
    if has_res:
        acc = acc + r_ref[...]
    o_ref[...] = acc.astype(o_ref.dtype)


def matmul(x, w, layer, col0, ncols, *, tm, tn, out_dtype=F32, residual=None, w_t=False):
    n, k = x.shape
    shift = col0 % LANES if w_t else 0
    col0 -= shift
    assert w.shape[2 if w_t else 1] == k and col0 % tn == 0 and ncols % tn == 0 and n % tm == 0
    assert shift % SUBLANES == 0
    c0 = col0 // tn
    if w_t:
        w_spec = pl.BlockSpec((None, tn, k), lambda j, i: (layer, c0 + j, 0))
    else:
        w_spec = pl.BlockSpec((None, k, tn), lambda j, i: (layer, 0, c0 + j))
    in_specs = [pl.BlockSpec((tm, k), lambda j, i: (i, 0)), w_spec]
    args = [x, w]
    if shift:
        per = tn // LANES
        nj = ncols // tn
        end = col0 + ncols
        assert end + shift <= w.shape[1]
        tail = jnp.pad(w[layer, end:end + shift, :], ((0, LANES - shift), (0, 0)))
        in_specs.append(pl.BlockSpec((None, LANES, k),
                                     lambda j, i: (layer, (c0 + jnp.minimum(j + 1, nj - 1)) * per, 0)))
        in_specs.append(pl.BlockSpec((LANES, k), lambda j, i: (0, 0)))
        args += [w, tail]
    if residual is not None:
        in_specs.append(pl.BlockSpec((tm, tn), lambda j, i: (i, j)))
        args.append(residual)
    return pl.pallas_call(
        functools.partial(_mm_body, has_res=residual is not None, w_t=w_t, shift=shift),
        out_shape=jax.ShapeDtypeStruct((n, ncols), out_dtype),
        grid=(ncols // tn, n // tm),
        in_specs=in_specs,
        out_specs=pl.BlockSpec((tm, tn), lambda j, i: (i, j)),
        scratch_shapes=[pltpu.VMEM((k, tn), BF16)],
        compiler_params=_cparams("arbitrary", "arbitrary"),
        name="matmul",
    )(*args)


def _merge_body(oa_ref, ob_ref, oc_ref, wa_ref, wb_ref, wc_ref, ga_ref, gb_ref, gc_ref, o_ref,
                wa_s, wb_s, wc_s):
    @pl.when(pl.program_id(1) == 0)
    def _():
        wa_s[...] = wa_ref[...].astype(BF16)
        wb_s[...] = wb_ref[...].astype(BF16)
        wc_s[...] = wc_ref[...].astype(BF16)

    m = _sigmoid(ga_ref[...]) * _dot(oa_ref[...].astype(BF16), wa_s[...])
    m = m + _sigmoid(gb_ref[...]) * _dot(ob_ref[...].astype(BF16), wb_s[...])
    m = m + _sigmoid(gc_ref[...]) * _dot(oc_ref[...].astype(BF16), wc_s[...])
    o_ref[...] = m.astype(o_ref.dtype)


def branch_merge(oa, ob, oc, wa, wb, wc, layer, gate, *, tm, tn):
    n, d = oa.shape
    nb = d // tn
    act = pl.BlockSpec((tm, d), lambda j, i: (i, 0))
    wsp = pl.BlockSpec((None, d, tn), lambda j, i: (layer, 0, j))
    gsp = [pl.BlockSpec((tm, tn), lambda j, i, b=b: (i, b * nb + j)) for b in range(3)]
    return pl.pallas_call(
        _merge_body,
        out_shape=jax.ShapeDtypeStruct((n, d), BF16),
        grid=(nb, n // tm),
        in_specs=[act, act, act, wsp, wsp, wsp] + gsp,
        out_specs=pl.BlockSpec((tm, tn), lambda j, i: (i, j)),
        scratch_shapes=[pltpu.VMEM((d, tn), BF16)] * 3,
        compiler_params=_cparams("arbitrary", "arbitrary"),
        name="branch_merge",
    )(oa, ob, oc, wa, wb, wc, gate, gate, gate)


def _lb_body(x_ref, o_ref):
    x = x_ref[...]
    depth = x.shape[0]
    mx = jnp.max(x, axis=0, keepdims=True)
    e = jnp.exp(x - mx)
    soft = e / jnp.sum(e, axis=0, keepdims=True)
    run = jnp.zeros_like(soft[0:1])
    o_ref[0:1, :] = run
    for l in range(1, depth):
        run = run + soft[l:l + 1]
        o_ref[l:l + 1, :] = run


def lower_bounds(logits):
    return pl.pallas_call(
        _lb_body, out_shape=jax.ShapeDtypeStruct(logits.shape, F32), name="hgrn_lower_bounds",
    )(logits)


def _logf_body(h_ref, w_ref, b_ref, lf_ref, c_ref, carry):
    t = h_ref.shape[0]
    rows = max(t, LANES)

    @pl.when(pl.program_id(1) == 0)
    def _():
        carry[...] = jnp.zeros_like(carry)

    z = _dot_nt(h_ref[...].astype(BF16), w_ref[...].astype(BF16)) + b_ref[...]
    lf = _log_sigmoid(z)
    lf_ref[...] = lf
    if rows > t:
        lf = jnp.concatenate([lf, jnp.zeros((rows - t, LANES), F32)], axis=0)
    r = lax.broadcasted_iota(jnp.int32, (rows, rows), 0)
    s = lax.broadcasted_iota(jnp.int32, (rows, rows), 1)
    tril = jnp.where(s <= r, 1.0, 0.0).astype(BF16)
    c = _dot_exact_left(tril, lf) + carry[...]
    c_ref[...] = c[:t]
    carry[...] = c[rows - 1:rows]


def fox_logf(h, w_in, layer, col_block, bias_pad, b, t, tt):
    n, k = h.shape
    nt = t // tt
    return pl.pallas_call(
        _logf_body,
        out_shape=(jax.ShapeDtypeStruct((n, LANES), F32), jax.ShapeDtypeStruct((n, LANES), F32)),
        grid=(b, nt),
        in_specs=[pl.BlockSpec((tt, k), lambda i, j: (i * nt + j, 0)),
                  pl.BlockSpec((None, LANES, k), lambda i, j: (layer, col_block, 0)),
                  pl.BlockSpec((1, LANES), lambda i, j: (0, 0))],
        out_specs=(pl.BlockSpec((tt, LANES), lambda i, j: (i * nt + j, 0)),
                   pl.BlockSpec((tt, LANES), lambda i, j: (i * nt + j, 0))),
        scratch_shapes=[pltpu.VMEM((1, LANES), F32)],
        compiler_params=_cparams("arbitrary", "arbitrary"),
        name="fox_logf",
    )(h, w_in, bias_pad)


def _hgrn_body(aq_ref, af_ref, ai_ref, ag_ref, lb_ref, ng_ref, s0_ref, o_ref, s_ref,
               st, g_s, q_s, k_s, v_s, o_s, *, hb):
    L, C = HGRN_CHUNK, HGRN_SUB
    tb = aq_ref.shape[0]
    c = pl.program_id(2)

    @pl.when(c == 0)
    def _():
        for h in range(hb):
            st[h] = s0_ref[0, h].T

    r = lax.broadcasted_iota(jnp.int32, (L, L), 0)
    s = lax.broadcasted_iota(jnp.int32, (L, L), 1)
    tril = jnp.where(s <= r, 1.0, 0.0).astype(BF16)
    row = lax.broadcasted_iota(jnp.int32, (L, 1), 0)

    for h in range(hb):
        ls = slice(h * LANES, (h + 1) * LANES)
        lb = lb_ref[:, ls]
        qr = aq_ref[:, ls]
        q = qr * _sigmoid(qr)
        fg = lb + (1.0 - lb) * _sigmoid(af_ref[:, ls])
        kk = 1.0 - fg
        gl = jnp.log(fg)
        v = ai_ref[:, ls]
        if tb < L:
            pad = jnp.zeros((L - tb, LANES), F32)
            q, kk, gl, v = (jnp.concatenate([a, pad], axis=0) for a in (q, kk, gl, v))

        G = _dot_exact_left(tril, gl)
        st_old = st[h]

        o = _dot_nt((q * jnp.exp(G)).astype(BF16), st_old.astype(BF16))

        a_off = jnp.zeros((L, L), F32)
        blk = C
        while blk < tb:
            grp = 2 * blk
            gref = jnp.concatenate(
                [jnp.broadcast_to(G[g0 + blk - 1:g0 + blk, :], (grp, LANES)) for g0 in range(0, L, grp)],
                axis=0)
            upper = (row // blk) % 2 == 1
            qt = q * jnp.exp(jnp.where(upper, G - gref, -jnp.inf))
            kt = kk * jnp.exp(jnp.where(upper, -jnp.inf, gref - G))
            a = _dot_nt(qt.astype(BF16), kt.astype(BF16))
            a_off = a_off + jnp.where(r // grp == s // grp, a, 0.0)
            blk = grp
        o_s[h] = o + _dot(a_off.astype(BF16), v.astype(BF16)) if tb > C else o

        g_s[h] = G
        q_s[h] = q
        k_s[h] = kk
        v_s[h] = v

        gl_last = G[L - 1:L, :]
        k_dec = kk * jnp.exp(gl_last - G)
        st[h] = st_old * jnp.exp(gl_last) + _dot(v.T.astype(BF16), k_dec.astype(BF16))

    ones = jnp.ones((LANES, LANES), BF16)
    rowc = lax.broadcasted_iota(jnp.int32, (C, LANES), 0)

    def sub_block(j, carry):
        base = pl.multiple_of(j * C, C)
        for h in range(hb):
            gj = g_s[h, pl.ds(base, C), :]
            qj = q_s[h, pl.ds(base, C), :]
            parts = []
            for si in range(C):
                gs = g_s[h, pl.ds(base + si, 1), :]
                ks = k_s[h, pl.ds(base + si, 1), :]
                e = jnp.exp(jnp.where(rowc >= si, gj - gs, -jnp.inf))
                parts.append(qj * e * ks)
            rs = _dot(jnp.concatenate(parts, axis=0).astype(BF16), ones)
            acc = o_s[h, pl.ds(base, C), :]
            for si in range(C):
                acc = acc + rs[si * C:(si + 1) * C] * v_s[h, pl.ds(base + si, 1), :]
            o_s[h, pl.ds(base, C), :] = acc
        return carry

    lax.fori_loop(0, -(-tb // C), sub_block, 0)

    for h in range(hb):
        ls = slice(h * LANES, (h + 1) * LANES)
        o = o_s[h][:tb]
        ms = jnp.mean(o * o, axis=-1, keepdims=True)
        o = o * lax.rsqrt(ms + EPS) * ng_ref[...] * _sigmoid(ag_ref[:, ls])
        o_ref[:, ls] = o.astype(o_ref.dtype)

    @pl.when(c == pl.num_programs(2) - 1)
    def _():
        for h in range(hb):
            s_ref[0, h] = st[h].T


def hgrn2(za, lb, norm_g, s0, b, t, out_dtype):
    n = za.shape[0]
    heads = s0.shape[1]
    assert s0.shape[2:] == (LANES, LANES)
    hb = min(HGRN_HEAD_BLOCK, heads)
    nhb = heads // hb
    tb = min(t, HGRN_CHUNK)
    nc = t // tb
    w = hb * LANES
    col = lambda part: pl.BlockSpec((tb, w), lambda i, h, c, part=part: (i * nc + c, part * nhb + h))
    return pl.pallas_call(
        functools.partial(_hgrn_body, hb=hb),
        out_shape=(jax.ShapeDtypeStruct((n, heads * LANES), out_dtype),
                   jax.ShapeDtypeStruct(s0.shape, F32)),
        grid=(b, nhb, nc),
        in_specs=[col(0), col(1), col(2), col(3),
                  pl.BlockSpec((1, w), lambda i, h, c: (0, h)),
                  pl.BlockSpec((1, LANES), lambda i, h, c: (0, 0)),
                  pl.BlockSpec((1, hb, LANES, LANES), lambda i, h, c: (i, h, 0, 0))],
        out_specs=(pl.BlockSpec((tb, w), lambda i, h, c: (i * nc + c, h)),
                   pl.BlockSpec((1, hb, LANES, LANES), lambda i, h, c: (i, h, 0, 0))),
        scratch_shapes=[pltpu.VMEM((hb, LANES, LANES), F32)] + [pltpu.VMEM((hb, HGRN_CHUNK, LANES), F32)] * 5,
        compiler_params=_cparams("arbitrary", "arbitrary", "arbitrary"),
        name="hgrn2",
    )(za, za, za, za, lb.reshape(1, -1), norm_g.reshape(1, -1), s0)


def _conv_body(x1_ref, x2_ref, buf_ref, w_ref, b_ref, lg_ref, lbias_ref, o_ref, new_ref, u_s, y_s, sh_s):
    tt, ch = x1_ref.shape
    width = w_ref.shape[0]
    hist = width - 1
    i = pl.program_id(1)

    @pl.when(i == 0)
    def _():
        u_s[0:CONV_HALO - hist, :] = jnp.zeros((CONV_HALO - hist, ch), F32)
        u_s[CONV_HALO - hist:CONV_HALO, :] = buf_ref[0]

    u_s[CONV_HALO:CONV_HALO + tt, :] = x1_ref[...] * _sigmoid(x2_ref[...])

    def lane_block(cb, carry):
        ls = pl.ds(pl.multiple_of(cb * LANES, LANES), LANES)
        acc = jnp.broadcast_to(b_ref[:, ls], (tt, LANES))
        for s in range(SUBLANES):
            taps = [j for j in range(width) if (CONV_HALO - hist + j) % SUBLANES == s]
            if not taps:
                continue
            span = max(CONV_HALO - hist + j - s for j in taps) + tt
            if s:
                sh_s[s, 0:span, :] = u_s[pl.ds(s, span), ls]
            for j in taps:
                off = CONV_HALO - hist + j - s
                xs = sh_s[s, off:off + tt, :] if s else u_s[off:off + tt, ls]
                acc = acc + w_ref[j:j + 1, ls] * xs
        y_s[:, ls] = acc
        return carry

    lax.fori_loop(0, ch // LANES, lane_block, 0)

    y = y_s[...]
    mu = jnp.mean(y, axis=-1, keepdims=True)
    yc = y - mu
    var = jnp.mean(yc * yc, axis=-1, keepdims=True)
    yn = yc * lax.rsqrt(var + EPS) * lg_ref[...] + lbias_ref[...]
    o_ref[...] = (yn * _sigmoid(yn)).astype(o_ref.dtype)

    @pl.when(i == pl.num_programs(1) - 1)
    def _():
        new_ref[0] = u_s[CONV_HALO + tt - hist:CONV_HALO + tt, :]

    if tt >= CONV_HALO:
        @pl.when(i < pl.num_programs(1) - 1)
        def _():
            u_s[0:CONV_HALO, :] = u_s[tt:tt + CONV_HALO, :]


def conv_branch(zb, buf, w, bias, ln_g, ln_b, b, t, tt, out_dtype):
    n = zb.shape[0]
    ch = zb.shape[1] // 2
    width = w.shape[0]
    nt = t // tt
    assert nt == 1 or tt >= CONV_HALO
    vec = pl.BlockSpec((1, ch), lambda i, j: (0, 0))
    return pl.pallas_call(
        _conv_body,
        out_shape=(jax.ShapeDtypeStruct((n, ch), out_dtype),
                   jax.ShapeDtypeStruct((b, width - 1, ch), F32)),
        grid=(b, nt),
        in_specs=[pl.BlockSpec((tt, ch), lambda i, j: (i * nt + j, 0)),
                  pl.BlockSpec((tt, ch), lambda i, j: (i * nt + j, 1)),
                  pl.BlockSpec((1, width - 1, ch), lambda i, j: (i, 0, 0)),
                  pl.BlockSpec((width, ch), lambda i, j: (0, 0)),
                  vec, vec, vec],
        out_specs=(pl.BlockSpec((tt, ch), lambda i, j: (i * nt + j, 0)),
                   pl.BlockSpec((1, width - 1, ch), lambda i, j: (i, 0, 0))),
        scratch_shapes=[pltpu.VMEM((CONV_HALO + tt, ch), F32), pltpu.VMEM((tt, ch), F32),
                        pltpu.VMEM((SUBLANES, CONV_HALO + tt, LANES), F32)],
        compiler_params=_cparams("arbitrary", "arbitrary"),
        name="conv_branch",
    )(zb, zb, buf, w, bias.reshape(1, ch), ln_g.reshape(1, ch), ln_b.reshape(1, ch))


def _fox_prompt_body(q_ref, k_ref, v_ref, c_ref, ct_ref, o_ref, *, tk, hb, dh):
    tq = q_ref.shape[0]
    scale = dh ** -0.5
    hp = pl.program_id(1)
    i = pl.program_id(2)
    lane = lax.broadcasted_iota(jnp.int32, (tq, LANES), 1)
    c_blk = c_ref[...]
    qs, cqs = [], []
    for hh in range(hb):
        qs.append((q_ref[:, hh * dh:(hh + 1) * dh] * scale).astype(BF16))
        cqs.append(jnp.sum(jnp.where(lane == hp * hb + hh, c_blk, 0.0), axis=1, keepdims=True))

    def kv_block(j, carry, masked):
        start = pl.multiple_of(j * tk, tk)
        out = []
        for hh in range(hb):
            m, l, acc = carry[hh]
            hs = slice(hh * dh, (hh + 1) * dh)
            ks = k_ref[pl.ds(start, tk), hs].astype(BF16)
            vs = v_ref[pl.ds(start, tk), hs].astype(BF16)
            ck = ct_ref[0, pl.ds(hp * hb + hh, 1), pl.ds(start, tk)]
            s = _dot_nt(qs[hh], ks) + cqs[hh] - ck
            if masked:
                qpos = i * tq + lax.broadcasted_iota(jnp.int32, (tq, tk), 0)
                kpos = j * tk + lax.broadcasted_iota(jnp.int32, (tq, tk), 1)
                s = jnp.where(kpos <= qpos, s, -jnp.inf)
            m_new = jnp.maximum(m, jnp.max(s, axis=1, keepdims=True))
            p = jnp.exp(s - m_new)
            alpha = jnp.exp(m - m_new)
            l = alpha * l + jnp.sum(p, axis=1, keepdims=True)
            acc = alpha * acc + _dot(p.astype(BF16), vs)
            out.append((m_new, l, acc))
        return tuple(out)

    n_full = (i * tq) // tk
    init = tuple((jnp.full((tq, 1), -jnp.inf, F32), jnp.zeros((tq, 1), F32), jnp.zeros((tq, dh), F32))
                 for _ in range(hb))
    carry = lax.fori_loop(0, n_full, lambda j, c: kv_block(j, c, False), init)
    carry = kv_block(n_full, carry, True)
    for hh in range(hb):
        _, l, acc = carry[hh]
        o_ref[:, hh * dh:(hh + 1) * dh] = (acc / l).astype(o_ref.dtype)


def fox_prompt(cq, ck, cv, c_pad, c_t, b, t, heads, tq, tk, hb):
    n = cq.shape[0]
    dh = cq.shape[1] // heads
    nq = t // tq
    assert tk % tq == 0 and t % tk == 0 and heads % hb == 0
    w = hb * dh
    return pl.pallas_call(
        functools.partial(_fox_prompt_body, tk=tk, hb=hb, dh=dh),
        out_shape=jax.ShapeDtypeStruct((n, heads * dh), BF16),
        grid=(b, heads // hb, nq),
        in_specs=[pl.BlockSpec((tq, w), lambda bi, h, i: (bi * nq + i, h)),
                  pl.BlockSpec((t, w), lambda bi, h, i: (bi, h)),
                  pl.BlockSpec((t, w), lambda bi, h, i: (bi, h)),
                  pl.BlockSpec((tq, LANES), lambda bi, h, i: (bi * nq + i, 0)),
                  pl.BlockSpec((1, c_t.shape[1], t), lambda bi, h, i: (bi, 0, 0))],
        out_specs=pl.BlockSpec((tq, w), lambda bi, h, i: (bi * nq + i, h)),
        compiler_params=_cparams("arbitrary", "arbitrary", "arbitrary"),
        name="fox_prompt",
    )(cq, ck, cv, c_pad, c_t)


def _fox_sample_body(pt_ref, q_ref, kn_ref, vn_ref, ccol_ref, crow_ref, *rest, heads, gp):
    kp_refs, vp_refs, lft_refs = rest[:gp], rest[gp:2 * gp], rest[2 * gp:3 * gp]
    o_ref, m_s, l_s, acc_s, carry_s = rest[3 * gp:]
    t, d = q_ref.shape
    dh = d // heads
    rows = lft_refs[0].shape[1]
    scale = dh ** -0.5
    p = pl.program_id(1)
    qb = [q_ref[:, h * dh:(h + 1) * dh].astype(BF16) for h in range(heads)]
    ccol = ccol_ref[0]

    def update(s, values_of):
        m_old = m_s[...]
        m_new = jnp.maximum(m_old, jnp.max(s, axis=1, keepdims=True))
        pe = jnp.exp(s - m_new)
        alpha = jnp.exp(m_old - m_new)
        l_s[...] = alpha * l_s[...] + jnp.sum(pe, axis=1, keepdims=True)
        pv = [_dot(pe[h * t:(h + 1) * t].astype(BF16), values_of(h)) for h in range(heads)]
        acc_s[...] = alpha * acc_s[...] + jnp.concatenate(pv, axis=0)
        m_s[...] = m_new

    @pl.when(p == 0)
    def _():
        carry_s[...] = jnp.zeros_like(carry_s)
        m_s[...] = jnp.full_like(m_s, -jnp.inf)
        l_s[...] = jnp.zeros_like(l_s)
        acc_s[...] = jnp.zeros_like(acc_s)
        pad = jnp.zeros((rows - t, dh), F32)
        sc = []
        for h in range(heads):
            kn = jnp.concatenate([kn_ref[:, h * dh:(h + 1) * dh], pad], axis=0)
            sc.append(_dot_nt(qb[h], kn.astype(BF16)) * scale)
        s = jnp.concatenate(sc, axis=0) + ccol - crow_ref[0]
        r = lax.broadcasted_iota(jnp.int32, (heads * t, rows), 0)
        c = lax.broadcasted_iota(jnp.int32, (heads * t, rows), 1)
        s = jnp.where(c <= r % t, s, -jnp.inf)
        update(s, lambda h: jnp.concatenate([vn_ref[:, h * dh:(h + 1) * dh], pad], axis=0).astype(BF16))

    rr = lax.broadcasted_iota(jnp.int32, (rows, rows), 0)
    cc = lax.broadcasted_iota(jnp.int32, (rows, rows), 1)
    later = jnp.where(rr > cc, 1.0, 0.0).astype(BF16)
    for g in range(gp):
        kp_ref, vp_ref = kp_refs[g], vp_refs[g]
        lft = lft_refs[g][...]
        d_t = _dot_exact_right(lft, later) + carry_s[...]
        carry_s[...] = carry_s[...] + jnp.sum(lft, axis=1, keepdims=True)
        sc = []
        for h in range(heads):
            kh = kp_ref[pl.ds(h, rows, stride=heads), :].astype(BF16)
            sc.append(_dot_nt(qb[h], kh) * scale + d_t[h:h + 1, :])
        s = jnp.concatenate(sc, axis=0) + ccol
        update(s, lambda h: vp_ref[pl.ds(h, rows, stride=heads), :].astype(BF16))

    @pl.when(p == pl.num_programs(1) - 1)
    def _():
        o = acc_s[...] / l_s[...]
        for h in range(heads):
            o_ref[:, h * dh:(h + 1) * dh] = o[h * t:(h + 1) * t]


def fox_sample(cq, ck_new, cv_new, c_col, c_row, cache_k, cache_v, cache_lft, layer, page_table, heads):
    n, d = cq.shape
    b, n_pages = page_table.shape
    t = n // b
    dh = d // heads
    rows = cache_lft.shape[3]
    gp = FOX_PAGES_PER_STEP
    assert rows == LANES and dh == LANES and n_pages % gp == 0
    tok = lambda bi, p, pt: (bi, 0)
    tab = lambda bi, p, pt: (bi, 0, 0)

    def page(g):
        return lambda bi, p, pt: (layer, pt[bi, n_pages - 1 - (gp * p + g)], 0, 0)

    kv_specs = [pl.BlockSpec((None, None, rows * heads, dh), page(g)) for g in range(gp)]
    lf_specs = [pl.BlockSpec((None, None, heads, rows), page(g)) for g in range(gp)]
    grid_spec = pltpu.PrefetchScalarGridSpec(
        num_scalar_prefetch=1,
        grid=(b, n_pages // gp),
        in_specs=[pl.BlockSpec((t, d), tok), pl.BlockSpec((t, d), tok), pl.BlockSpec((t, d), tok),
                  pl.BlockSpec((1, heads * t, LANES), tab), pl.BlockSpec((1, heads * t, LANES), tab)]
        + kv_specs + kv_specs + lf_specs,
        out_specs=pl.BlockSpec((t, d), tok),
        scratch_shapes=[pltpu.VMEM((heads * t, LANES), F32), pltpu.VMEM((heads * t, LANES), F32),
                        pltpu.VMEM((heads * t, dh), F32), pltpu.VMEM((heads, LANES), F32)],
    )
    return pl.pallas_call(
        functools.partial(_fox_sample_body, heads=heads, gp=gp),
        out_shape=jax.ShapeDtypeStruct((n, d), F32),
        grid_spec=grid_spec,
        compiler_params=_cparams("arbitrary", "arbitrary"),
        name="fox_sample",
    )(page_table, cq, ck_new, cv_new, c_col, c_row, *([cache_k] * gp), *([cache_v] * gp), *([cache_lft] * gp))


def _mem_attn_body(q_ref, k_ref, v_ref, o_ref, *, heads):
    d = q_ref.shape[1]
    dh = d // heads
    scale = dh ** -0.5
    for h in range(heads):
        hs = slice(h * dh, (h + 1) * dh)
        s = _dot_nt(q_ref[:, hs].astype(BF16), k_ref[0, :, hs].astype(BF16)) * scale
        m = jnp.max(s, axis=1, keepdims=True)
        pe = jnp.exp(s - m)
        o = _dot(pe.astype(BF16), v_ref[0, :, hs].astype(BF16)) / jnp.sum(pe, axis=1, keepdims=True)
        o_ref[:, hs] = o.astype(o_ref.dtype)


def mem_attention(qm, mem_k, mem_v, b, t, tt, heads, out_dtype):
    n, d = qm.shape
    nm = mem_k.shape[1]
    nt = t // tt
    return pl.pallas_call(
        functools.partial(_mem_attn_body, heads=heads),
        out_shape=jax.ShapeDtypeStruct((n, d), out_dtype),
        grid=(b, nt),
        in_specs=[pl.BlockSpec((tt, d), lambda i, j: (i * nt + j, 0)),
                  pl.BlockSpec((1, nm, d), lambda i, j: (i, 0, 0)),
                  pl.BlockSpec((1, nm, d), lambda i, j: (i, 0, 0))],
        out_specs=pl.BlockSpec((tt, d), lambda i, j: (i * nt + j, 0)),
        compiler_params=_cparams("arbitrary", "arbitrary"),
        name="mem_attention",
    )(qm, mem_k, mem_v)


def _ffn_act_body(g_ref, v_ref, buf_ref, w_ref, b_ref, o_ref, new_ref, g_s, sh_s):
    t, tc = g_ref.shape
    width = w_ref.shape[0]
    hist = width - 1
    g_s[FFN_HALO - hist:FFN_HALO, :] = buf_ref[0]
    g_s[FFN_HALO:FFN_HALO + t, :] = g_ref[...]
    acc = jnp.broadcast_to(b_ref[...], (t, tc))
    for j in range(width):
        start = FFN_HALO - hist + j
        if start % SUBLANES:
            sh_s[j] = g_s[pl.ds(start, t), :]
            xs = sh_s[j]
        else:
            xs = g_s[start:start + t, :]
        acc = acc + w_ref[j:j + 1, :] * xs
    o_ref[...] = (acc * _sigmoid(acc) * v_ref[...]).astype(o_ref.dtype)
    new_ref[0] = g_s[FFN_HALO + t - hist:FFN_HALO + t, :]


def ffn_act(up, buf, w, bias, b, t, tc, out_dtype):
    n = up.shape[0]
    dff = up.shape[1] // 2
    width = w.shape[0]
    nj = dff // tc
    return pl.pallas_call(
        _ffn_act_body,
        out_shape=(jax.ShapeDtypeStruct((n, dff), out_dtype),
                   jax.ShapeDtypeStruct((b, width - 1, dff), F32)),
        grid=(b, nj),
        in_specs=[pl.BlockSpec((t, tc), lambda i, j: (i, j)),
                  pl.BlockSpec((t, tc), lambda i, j: (i, nj + j)),
                  pl.BlockSpec((1, width - 1, tc), lambda i, j: (i, 0, j)),
                  pl.BlockSpec((width, tc), lambda i, j: (0, j)),
                  pl.BlockSpec((1, tc), lambda i, j: (0, j))],
        out_specs=(pl.BlockSpec((t, tc), lambda i, j: (i, j)),
                   pl.BlockSpec((1, width - 1, tc), lambda i, j: (i, 0, j))),
        scratch_shapes=[pltpu.VMEM((FFN_HALO + t, tc), F32), pltpu.VMEM((width, t, tc), F32)],
        compiler_params=_cparams("arbitrary", "arbitrary"),
        name="ffn_act",
    )(up, up, buf, w, bias.reshape(1, dff))


def _proj_tiles(n, k):
    if k <= 2048:
        return min(n, 1024), 1024
    return min(n, 512), 512


def _trunk_layer(x, b, t, l, w, p, lb, mem_k, mem_v, mem_heads, s0, conv_buf, ffn_buf, attend, prompt):
    n, d = x.shape
    heads_a = s0.shape[1]
    wa_cols = heads_a * LANES
    act = BF16 if prompt else F32
    tm, tn = _proj_tiles(n, d)
    w_in = w['w_in']
    off_b = 4 * wa_cols
    off_c = off_b + 2 * d
    off_f = off_c + 3 * d
    fox_heads = p['fox_f_bias'].shape[0]

    h1 = rmsnorm(x, p['norm_mix_g'], act, min(n, 512))
    za = matmul(h1, w_in, l, 0, off_b, tm=tm, tn=tn, w_t=True)
    zb = matmul(h1, w_in, l, off_b, 2 * d, tm=tm, tn=tn, w_t=True)
    cq = matmul(h1, w_in, l, off_c, d, tm=tm, tn=tn, w_t=True)
    ck = matmul(h1, w_in, l, off_c + d, d, tm=tm, tn=tn, w_t=True)
    cv = matmul(h1, w_in, l, off_c + 2 * d, d, tm=tm, tn=tn, w_t=True)
    gate = matmul(h1, w_in, l, off_f + fox_heads, 3 * d, tm=tm, tn=tn, w_t=True)
    bias_pad = jnp.zeros((1, LANES), F32).at[0, :fox_heads].set(p['fox_f_bias'])
    logf_pad, c_pad = fox_logf(h1, w_in, l, off_f // LANES, bias_pad, b, t, min(t, 256))

    oa, s_new = hgrn2(za, lb, p['hgrn_norm_g'], s0, b, t, act)
    ob, conv_new = conv_branch(zb, conv_buf, p['conv_dw_w'], p['conv_dw_b'], p['conv_ln_g'], p['conv_ln_b'],
                               b, t, min(t, 256), act)
    c_t = c_pad[:, :fox_heads].reshape(b, t, fox_heads).transpose(0, 2, 1)
    oc = attend(cq, ck, cv, c_pad, c_t)

    m = branch_merge(oa, ob, oc, w['w_branch_a'], w['w_branch_b'], w['w_branch_c'], l, gate,
                     tm=min(n, 256), tn=512)
    x = matmul(m, w['w_out'], l, 0, d, tm=tm, tn=tn, residual=x)

    h2 = rmsnorm(x, p['norm_mem_g'], act, min(n, 512))
    qm = matmul(h2, w['w_mq'], l, 0, d, tm=tm, tn=tn, out_dtype=act)
    om = mem_attention(qm, mem_k, mem_v, b, t, min(t, 512), mem_heads, act)
    x = matmul(om, w['w_mo'], l, 0, d, tm=tm, tn=tn, residual=x)

    h3 = rmsnorm(x, p['norm_ffn_g'], act, min(n, 512))
    dff = w['w_down'].shape[1]
    up = matmul(h3, w['w_up'], l, 0, 2 * dff, tm=tm, tn=tn)
    a, ffn_new = ffn_act(up, ffn_buf, p['ffn_dw_w'], p['ffn_dw_b'], b, t, 512, act)
    tm_d, tn_d = _proj_tiles(n, dff)
    x = matmul(a, w['w_down'], l, 0, d, tm=tm_d, tn=tn_d, residual=x)

    logf = logf_pad[:, :fox_heads].reshape(b, t, fox_heads)
    return x, (ck, cv, logf, s_new, conv_new, ffn_new)


def kernel(x_prompt, x_sample, cache_k, cache_v, cache_logf, page_table, cache_mem_k, cache_mem_v, state_hgrn, state_conv, state_ffn, mem_prompt, norm_mix_g, w_in, hgrn_lb_logits, hgrn_norm_g, conv_dw_w, conv_dw_b, conv_ln_g, conv_ln_b, fox_f_bias, w_branch_a, w_branch_b, w_branch_c, w_out, norm_mem_g, mem_kv_norm_g, w_mq, w_mk, w_mv, w_mo, norm_ffn_g, w_up, ffn_dw_w, ffn_dw_b, w_down, final_norm_g):
    depth = w_in.shape[0]
    bp, tp, d = x_prompt.shape
    bs, ts, _ = x_sample.shape
    n_mem = mem_prompt.shape[1]
    fox_heads = fox_f_bias.shape[1]
    dh = d // fox_heads
    hgrn_heads = state_hgrn.shape[2]
    n_pool, page_rows = cache_k.shape[1], cache_k.shape[2]
    mem_heads = cache_mem_k.shape[3]

    w = {'w_in': jnp.swapaxes(w_in, 1, 2), 'w_branch_a': w_branch_a, 'w_branch_b': w_branch_b,
         'w_branch_c': w_branch_c, 'w_out': w_out, 'w_mq': w_mq, 'w_mo': w_mo, 'w_up': w_up, 'w_down': w_down}

    ck_pages = cache_k.reshape(depth, n_pool, page_rows * fox_heads, dh)
    cv_pages = cache_v.reshape(depth, n_pool, page_rows * fox_heads, dh)
    lft_pages = cache_logf.astype(F32).transpose(0, 1, 3, 2)

    lbs = lower_bounds(hgrn_lb_logits.astype(F32))
    xp = x_prompt.reshape(bp * tp, d)
    xs = x_sample.reshape(bs * ts, d)
    memp = mem_prompt.reshape(bp * n_mem, d)
    rows_p, rows_s, mem_rows = [], [], []
    for l in range(depth):
        p = {'norm_mix_g': norm_mix_g[l], 'hgrn_norm_g': hgrn_norm_g[l],
             'conv_dw_w': conv_dw_w[l], 'conv_dw_b': conv_dw_b[l], 'conv_ln_g': conv_ln_g[l],
             'conv_ln_b': conv_ln_b[l], 'fox_f_bias': fox_f_bias[l],
             'norm_mem_g': norm_mem_g[l], 'norm_ffn_g': norm_ffn_g[l],
             'ffn_dw_w': ffn_dw_w[l], 'ffn_dw_b': ffn_dw_b[l]}
        lb = lbs[l]

        mn = rmsnorm(memp, mem_kv_norm_g[l], BF16, 512)
        tm_m, tn_m = _proj_tiles(bp * n_mem, d)
        mk = matmul(mn, w_mk, l, 0, d, tm=tm_m, tn=tn_m)
        mv = matmul(mn, w_mv, l, 0, d, tm=tm_m, tn=tn_m)
        mem_rows.append((mk, mv))
        s0 = jnp.zeros((bp, hgrn_heads, LANES, LANES), F32)
        cb0 = jnp.zeros((bp, conv_dw_w.shape[1] - 1, d), F32)
        fb0 = jnp.zeros((bp, ffn_dw_w.shape[1] - 1, w_down.shape[1]), F32)
        attend_p = lambda cq, ck, cv, c_pad, c_t: fox_prompt(cq, ck, cv, c_pad, c_t, bp, tp, fox_heads, 256, 512, 4)
        xp, new_p = _trunk_layer(xp, bp, tp, l, w, p, lb, mk.reshape(bp, n_mem, d), mv.reshape(bp, n_mem, d),
                                 mem_heads, s0, cb0, fb0, attend_p, True)
        rows_p.append(new_p)

        def attend_s(cq, ck, cv, c_pad, c_t, l=l):
            c_col = jnp.broadcast_to(c_t.reshape(bs, fox_heads * ts, 1), (bs, fox_heads * ts, LANES))
            c_row = jnp.repeat(jnp.pad(c_t, ((0, 0), (0, 0), (0, LANES - ts))), ts, axis=1)
            return fox_sample(cq, ck, cv, c_col, c_row, ck_pages, cv_pages, lft_pages, l, page_table, fox_heads)

        xs, new_s = _trunk_layer(xs, bs, ts, l, w, p, lb, cache_mem_k[l].reshape(bs, n_mem, d),
                                 cache_mem_v[l].reshape(bs, n_mem, d), mem_heads, state_hgrn[l],
                                 state_conv[l], state_ffn[l], attend_s, False)
        rows_s.append(new_s)

    y_prompt = rmsnorm(xp, final_norm_g, F32, 512).reshape(bp, tp, d)
    y_sample = rmsnorm(xs, final_norm_g, F32, bs * ts).reshape(bs, ts, d)

    def stack(rows, i, shape):
        return jnp.stack([r[i].reshape(shape) for r in rows])

    k_prompt = stack(rows_p, 0, (bp, tp, fox_heads, dh))
    v_prompt = stack(rows_p, 1, (bp, tp, fox_heads, dh))
    logf_prompt = stack(rows_p, 2, (bp, tp, fox_heads))
    hgrn_prompt = stack(rows_p, 3, (bp, hgrn_heads, LANES, LANES))
    conv_prompt = stack(rows_p, 4, (bp, conv_dw_w.shape[1] - 1, d))
    ffn_prompt = stack(rows_p, 5, (bp, ffn_dw_w.shape[1] - 1, w_down.shape[1]))
    mem_k_prompt = stack(mem_rows, 0, (bp, n_mem, mem_heads, d // mem_heads))
    mem_v_prompt = stack(mem_rows, 1, (bp, n_mem, mem_heads, d // mem_heads))
    k_sample = stack(rows_s, 0, (bs, ts, fox_heads, dh))
    v_sample = stack(rows_s, 1, (bs, ts, fox_heads, dh))
    logf_sample = stack(rows_s, 2, (bs, ts, fox_heads))
    hgrn_sample = stack(rows_s, 3, (bs, hgrn_heads, LANES, LANES))
    conv_sample = stack(rows_s, 4, (bs, conv_dw_w.shape[1] - 1, d))
    ffn_sample = stack(rows_s, 5, (bs, ffn_dw_w.shape[1] - 1, w_down.shape[1]))
    return (y_prompt, y_sample, k_prompt, v_prompt, logf_prompt, mem_k_prompt, mem_v_prompt, hgrn_prompt,
            conv_prompt, ffn_prompt, k_sample, v_sample, logf_sample, hgrn_sample, conv_sample, ffn_sample)
```
